```python
import math
import jax, jax.numpy as jnp
from jax import lax
import numpy as np

D_MODEL = 1024
BATCH = 32
SEQ = 2048
DEPTH = 1

HEAD_DIM = 64
SB_HEADS = 8
MOBA_HEADS = 8
SB_WIDTH = SB_HEADS * HEAD_DIM
MOBA_WIDTH = MOBA_HEADS * HEAD_DIM
IN_WIDTH = 3 * SB_WIDTH + 3 * MOBA_WIDTH
SB_BLOCK = 128
MOBA_BLOCK = 256
MOBA_TOPK = 3
MOBA_Q_CHUNK = 16
NUM_BUCKETS = 32
MAX_EXACT = NUM_BUCKETS // 2
MAX_DISTANCE = 128
FFN_HIDDEN = int(math.ceil(8 * D_MODEL / 3 / 256) * 256)
PLE_DIM = 256
RMS_EPS = 1e-6
NEG = -1e30

kernel_name = "hybrid_stickbreak_moba_gated_block"


def rmsnorm(x, g):
    xf = x.astype(jnp.float32)
    y = xf * lax.rsqrt(jnp.mean(xf * xf, axis=-1, keepdims=True) + RMS_EPS)
    return (y * g.astype(jnp.float32)).astype(x.dtype)


def split_heads(t, n_heads):
    b, s, _ = t.shape
    return t.reshape(b, s, n_heads, HEAD_DIM).transpose(0, 2, 1, 3)


def merge_heads(t):
    b, h, s, d = t.shape
    return t.transpose(0, 2, 1, 3).reshape(b, s, h * d)


def t5_bucket(dist):
    n = jnp.maximum(dist, 0)
    nf = jnp.maximum(n, 1).astype(jnp.float32)
    large = MAX_EXACT + (jnp.log(nf / MAX_EXACT) / math.log(MAX_DISTANCE / MAX_EXACT)
                         * (NUM_BUCKETS - MAX_EXACT)).astype(jnp.int32)
    large = jnp.minimum(large, NUM_BUCKETS - 1)
    return jnp.where(n < MAX_EXACT, n, large)


def stick_breaking_attention(q, k, v):
    S = q.shape[2]
    scale = HEAD_DIM ** -0.5
    outs = []
    for i in range(S // SB_BLOCK):
        t0 = i * SB_BLOCK
        L = t0 + SB_BLOCK
        z = jnp.einsum('bhtd,bhsd->bhts', q[:, :, t0:L], k[:, :, :L]).astype(jnp.float32) * scale
        tpos = t0 + jnp.arange(SB_BLOCK)
        spos = jnp.arange(L)
        causal = spos[None, :] < tpos[:, None]
        log_fail = jnp.where(causal, jax.nn.log_sigmoid(-z), 0.0)
        suffix = lax.cumsum(log_fail, axis=3, reverse=True) - log_fail
        w = jnp.where(causal, jnp.exp(jax.nn.log_sigmoid(z) + suffix), 0.0)
        outs.append(jnp.einsum('bhts,bhsd->bhtd', w.astype(v.dtype), v[:, :, :L]))
    return jnp.concatenate(outs, axis=2)


_gather_blocks = jax.vmap(jax.vmap(lambda blk, idx: blk[idx]))


def moba_attention(q, k, v, rel_bias):
    B, H, S, dh = q.shape
    scale = HEAD_DIM ** -0.5
    nb = -(-S // MOBA_BLOCK)
    s_pad = nb * MOBA_BLOCK
    pad = ((0, 0), (0, 0), (0, s_pad - S), (0, 0))
    kb = jnp.pad(k, pad).reshape(B, H, nb, MOBA_BLOCK, dh)
    vb = jnp.pad(v, pad).reshape(B, H, nb, MOBA_BLOCK, dh)
    kbar = jnp.mean(kb.astype(jnp.float32), axis=3)
    gate = jnp.einsum('bhtd,bhnd->bhtn', q.astype(jnp.float32), kbar)
    cur_blk = jnp.arange(S) // MOBA_BLOCK
    past = jnp.arange(nb)[None, :] < cur_blk[:, None]
    gate = jnp.where(past, gate, -jnp.inf)
    k_sel = min(MOBA_TOPK, nb)
    _, sel = lax.top_k(gate, k_sel)
    sel_valid = sel < cur_blk[:, None]
    bias_t = rel_bias.T.astype(jnp.float32)
    h_idx = jnp.arange(H)[None, :, None, None, None]
    u = jnp.arange(MOBA_BLOCK)

    def chunk(ci):
        t0 = ci * MOBA_Q_CHUNK
        qc = lax.dynamic_slice_in_dim(q, t0, MOBA_Q_CHUNK, axis=2)
        selc = lax.dynamic_slice_in_dim(sel, t0, MOBA_Q_CHUNK, axis=2)
        validc = lax.dynamic_slice_in_dim(sel_valid, t0, MOBA_Q_CHUNK, axis=2)
        tpos = t0 + jnp.arange(MOBA_Q_CHUNK)
        c = t0 // MOBA_BLOCK
        own_k = lax.dynamic_index_in_dim(kb, c, axis=2, keepdims=False)
        own_v = lax.dynamic_index_in_dim(vb, c, axis=2, keepdims=False)
        spos_own = c * MOBA_BLOCK + u
        d_own = tpos[:, None] - spos_own[None, :]
        l_own = jnp.einsum('bhtd,bhsd->bhts', qc, own_k).astype(jnp.float32) * scale
        l_own = l_own + bias_t[:, t5_bucket(d_own)][None]
        l_own = jnp.where((d_own >= 0)[None, None], l_own, NEG)
        k_g = _gather_blocks(kb, selc)
        v_g = _gather_blocks(vb, selc)
        l_sel = jnp.einsum('bhtd,bhtjsd->bhtjs', qc, k_g).astype(jnp.float32) * scale
        d_sel = tpos[None, None, :, None, None] - (selc[..., None] * MOBA_BLOCK + u)
        l_sel = l_sel + bias_t[h_idx, t5_bucket(d_sel)]
        l_sel = jnp.where(validc[..., None], l_sel, NEG)
        kk = l_sel.shape[3]
        logits = jnp.concatenate([l_sel.reshape(B, H, MOBA_Q_CHUNK, kk * MOBA_BLOCK), l_own], axis=-1)
        probs = jax.nn.softmax(logits, axis=-1).astype(v.dtype)
        p_sel = probs[..., :kk * MOBA_BLOCK].reshape(B, H, MOBA_Q_CHUNK, kk, MOBA_BLOCK)
        p_own = probs[..., kk * MOBA_BLOCK:]
        return (jnp.einsum('bhtjs,bhtjsd->bhtd', p_sel, v_g)
                + jnp.einsum('bhts,bhsd->bhtd', p_own, own_v))

    outs = lax.map(chunk, jnp.arange(S // MOBA_Q_CHUNK))
    return outs.transpose(1, 2, 0, 3, 4).reshape(B, H, S, dh)


def setup_inputs(seed: int = 0) -> dict:
    key = jax.random.key(seed)
    ks = jax.random.split(key, 20)
    f32 = jnp.float32

    def w(k, shape, fan_in):
        return jax.random.normal(k, shape, f32) * (fan_in ** -0.5)

    def gain(k, shape):
        return 1.0 + 0.05 * jax.random.normal(k, shape, f32)

    return {
        "x": jax.random.normal(ks[0], (BATCH, SEQ, D_MODEL), f32),
        "p": jax.random.normal(ks[1], (DEPTH, BATCH, SEQ, PLE_DIM), f32),
        "ln_mix_g": gain(ks[2], (DEPTH, D_MODEL)),
        "w_in": w(ks[3], (DEPTH, D_MODEL, IN_WIDTH), D_MODEL),
        "w_gate": w(ks[4], (DEPTH, D_MODEL, 2 * D_MODEL), D_MODEL),
        "b_gate": 0.02 * jax.random.normal(ks[5], (DEPTH, 2 * D_MODEL), f32),
        "w_branch_sb": w(ks[6], (DEPTH, SB_WIDTH, D_MODEL), SB_WIDTH),
        "w_branch_moba": w(ks[7], (DEPTH, MOBA_WIDTH, D_MODEL), MOBA_WIDTH),
        "w_out": w(ks[8], (DEPTH, D_MODEL, D_MODEL), D_MODEL),
        "rel_bias": 0.5 * jax.random.normal(ks[9], (NUM_BUCKETS, MOBA_HEADS), f32),
        "ln_ffn_g": gain(ks[10], (DEPTH, D_MODEL)),
        "w_ffn_gate": w(ks[11], (DEPTH, D_MODEL, FFN_HIDDEN), D_MODEL),
        "w_ffn_up": w(ks[12], (DEPTH, D_MODEL, FFN_HIDDEN), D_MODEL),
        "w_ffn_down": w(ks[13], (DEPTH, FFN_HIDDEN, D_MODEL), FFN_HIDDEN),
        "ln_ple_g": gain(ks[14], (DEPTH, D_MODEL)),
        "w_ple_gate": w(ks[15], (DEPTH, D_MODEL, D_MODEL), D_MODEL),
        "w_ple_proj": w(ks[16], (DEPTH, PLE_DIM, D_MODEL), PLE_DIM),
        "final_g": gain(ks[17], (D_MODEL,)),
    }


def reference(x, p, ln_mix_g, w_in, w_gate, b_gate, w_branch_sb, w_branch_moba, w_out,
              rel_bias, ln_ffn_g, w_ffn_gate, w_ffn_up, w_ffn_down, ln_ple_g, w_ple_gate,
              w_ple_proj, final_g):
    for i in range(DEPTH):
        h = rmsnorm(x, ln_mix_g[i])
        proj = h @ w_in[i]
        q_sb, k_sb, v_sb, q_mb, k_mb, v_mb = jnp.split(
            proj, np.cumsum([SB_WIDTH] * 3 + [MOBA_WIDTH] * 2).tolist(), axis=-1)
        o_sb = stick_breaking_attention(split_heads(q_sb, SB_HEADS), split_heads(k_sb, SB_HEADS),
                                        split_heads(v_sb, SB_HEADS))
        o_mb = moba_attention(split_heads(q_mb, MOBA_HEADS), split_heads(k_mb, MOBA_HEADS),
                              split_heads(v_mb, MOBA_HEADS), rel_bias)
        y_sb = merge_heads(o_sb) @ w_branch_sb[i]
        y_mb = merge_heads(o_mb) @ w_branch_moba[i]
        g_sb, g_mb = jnp.split(jax.nn.sigmoid(h @ w_gate[i] + b_gate[i]), 2, axis=-1)
        x = x + (g_sb * y_sb + g_mb * y_mb) @ w_out[i]
        h2 = rmsnorm(x, ln_ffn_g[i])
        x = x + (jax.nn.silu(h2 @ w_ffn_gate[i]) * (h2 @ w_ffn_up[i])) @ w_ffn_down[i]
        g_ple = jax.nn.sigmoid(rmsnorm(x, ln_ple_g[i]) @ w_ple_gate[i])
        x = x + g_ple * (p[i] @ w_ple_proj[i])
    return rmsnorm(x, final_g)
```

```python
import functools
import math

import jax
import jax.numpy as jnp
import numpy as np
from jax import lax
from jax.experimental import pallas as pl
from jax.experimental.pallas import tpu as pltpu

HEAD_DIM = 64
N_HEADS = 8
MIX_WIDTH = N_HEADS * HEAD_DIM
HEADS_PER_GROUP = 4
GROUP_WIDTH = HEADS_PER_GROUP * HEAD_DIM
N_GROUPS = N_HEADS // HEADS_PER_GROUP
BLK = 256
MOBA_TOPK = 3
NUM_BUCKETS = 32
MAX_EXACT = NUM_BUCKETS // 2
MAX_DISTANCE = 128
RMS_EPS = 1e-6
NEG = -1e30
VMEM_LIMIT_BYTES = 56 * 1024 * 1024

F32 = jnp.float32
BF16 = jnp.bfloat16
NT_DIMS = (((1,), (1,)), ((), ()))


def _rms(x, g):
    return x * lax.rsqrt(jnp.mean(x * x, axis=-1, keepdims=True) + RMS_EPS) * g


def _in_proj_kernel(x_ref, g_ref, wqk_ref, wvt_ref, wg_ref, bg_ref, qk_ref, vt_ref, gate_ref):
    h = _rms(x_ref[...], g_ref[...]).astype(BF16)
    qk_ref[...] = jnp.dot(h, wqk_ref[...], preferred_element_type=F32).astype(BF16)
    vt_ref[...] = lax.dot_general(wvt_ref[...], h, NT_DIMS, preferred_element_type=F32).astype(BF16)
    gl = jnp.dot(h, wg_ref[...], preferred_element_type=F32) + bg_ref[...]
    gate_ref[...] = jax.nn.sigmoid(gl).astype(BF16)


def _in_proj(x, ln_g, w_qk, w_vt, w_gate, b_gate, tm):
    b, s, d = x.shape
    n_qk, n_v, n_g = w_qk.shape[1], w_vt.shape[0], w_gate.shape[1]

    def const(shape):
        return pl.BlockSpec(shape, lambda bi, ti: (0, 0), pipeline_mode=pl.Buffered(1))

    return pl.pallas_call(
        _in_proj_kernel,
        grid=(b, s // tm),
        in_specs=[
            pl.BlockSpec((None, tm, d), lambda bi, ti: (bi, ti, 0)),
            const((1, d)),
            const((d, n_qk)),
            const((n_v, d)),
            const((d, n_g)),
            const((1, n_g)),
        ],
        out_specs=[
            pl.BlockSpec((None, tm, n_qk), lambda bi, ti: (bi, ti, 0)),
            pl.BlockSpec((None, n_v, tm), lambda bi, ti: (bi, 0, ti)),
            pl.BlockSpec((None, tm, n_g), lambda bi, ti: (bi, ti, 0)),
        ],
        out_shape=[
            jax.ShapeDtypeStruct((b, s, n_qk), BF16),
            jax.ShapeDtypeStruct((b, n_v, s), BF16),
            jax.ShapeDtypeStruct((b, s, n_g), BF16),
        ],
        compiler_params=pltpu.CompilerParams(
            dimension_semantics=("arbitrary", "arbitrary"), vmem_limit_bytes=VMEM_LIMIT_BYTES),
        name="in_proj",
    )(x, ln_g, w_qk, w_vt, w_gate, b_gate)


def _head_stack_keys(kb):
    lane_head = lax.broadcasted_iota(jnp.int32, kb.shape, 1) // HEAD_DIM
    zero = jnp.zeros_like(kb)
    return jnp.concatenate([jnp.where(lane_head == h, kb, zero) for h in range(HEADS_PER_GROUP)], axis=0)


def _head_rows(vt, h):
    row_head = lax.broadcasted_iota(jnp.int32, vt.shape, 0) // HEAD_DIM
    return jnp.where(row_head == h, vt, jnp.zeros_like(vt))


def _sb_kernel(q_ref, k_ref, vt_ref, tri_ref, o_ref, acc_ref):
    n_blk = q_ref.shape[0] // BLK
    tri = tri_ref[...]
    s_idx = lax.broadcasted_iota(jnp.int32, (BLK, BLK), 0)
    t_idx = lax.broadcasted_iota(jnp.int32, (BLK, BLK), 1)
    causal = s_idx < t_idx

    def tile(qb, j, carry, diagonal):
        kb = k_ref[pl.ds(pl.multiple_of(j * BLK, BLK), BLK), :]
        vtb = vt_ref[:, pl.ds(pl.multiple_of(j * BLK, BLK), BLK)]
        z = lax.dot_general(_head_stack_keys(kb), qb, NT_DIMS, preferred_element_type=F32)
        new_carry = []
        for h in range(HEADS_PER_GROUP):
            zh = z[h * BLK:(h + 1) * BLK]
            a = jnp.maximum(zh, 0.0) + jnp.log(1.0 + jnp.exp(-jnp.abs(zh)))
            if diagonal:
                a = jnp.where(causal, a, 0.0)
            hi = a.astype(BF16)
            lo = (a - hi.astype(F32)).astype(BF16)
            suf = jnp.dot(tri, jnp.concatenate([hi, lo], axis=0), preferred_element_type=F32)
            w = jnp.exp(zh - a - suf - carry[h])
            if diagonal:
                w = jnp.where(causal, w, 0.0)
            acc_ref[...] += jnp.dot(_head_rows(vtb, h), w.astype(BF16), preferred_element_type=F32)
            new_carry.append(carry[h] + suf[0:1, :] + a[0:1, :])
        return tuple(new_carry)

    def q_block(i, _):
        qb = q_ref[pl.ds(pl.multiple_of(i * BLK, BLK), BLK), :]
        acc_ref[...] = jnp.zeros_like(acc_ref)
        carry = tile(qb, i, tuple(jnp.zeros((1, BLK), F32) for _ in range(HEADS_PER_GROUP)), True)
        lax.fori_loop(0, i, lambda jj, c: tile(qb, i - 1 - jj, c, False), carry)
        o_ref[pl.ds(pl.multiple_of(i * BLK, BLK), BLK), :] = acc_ref[...].T.astype(BF16)
        return 0

    lax.fori_loop(0, n_blk, q_block, 0)


def _sb_attention(qk, vt, tri):
    b, s, _ = qk.shape
    return pl.pallas_call(
        _sb_kernel,
        grid=(b, N_GROUPS),
        in_specs=[
            pl.BlockSpec((None, s, GROUP_WIDTH), lambda bi, gi: (bi, 0, gi)),
            pl.BlockSpec((None, s, GROUP_WIDTH), lambda bi, gi: (bi, 0, N_GROUPS + gi)),
            pl.BlockSpec((None, GROUP_WIDTH, s), lambda bi, gi: (bi, gi, 0)),
            pl.BlockSpec((BLK, 2 * BLK), lambda bi, gi: (0, 0)),
        ],
        out_specs=pl.BlockSpec((None, s, GROUP_WIDTH), lambda bi, gi: (bi, 0, gi)),
        out_shape=jax.ShapeDtypeStruct((b, s, MIX_WIDTH), BF16),
        scratch_shapes=[pltpu.VMEM((GROUP_WIDTH, BLK), F32)],
        compiler_params=pltpu.CompilerParams(
            dimension_semantics=("arbitrary", "arbitrary"), vmem_limit_bytes=VMEM_LIMIT_BYTES),
        name="sb_attn",
    )(qk, qk, vt, tri)


def _moba_kernel(bias_ref, q_ref, k_ref, vt_ref, bkt_ref, o_ref,
                 btile_ref, madd_ref, logit_ref, cmax_ref, acc_ref):
    gi = pl.program_id(0)
    bi = pl.program_id(1)
    s_len = q_ref.shape[0]
    n_blk = s_len // BLK
    far_bucket = NUM_BUCKETS - 1

    @pl.when(bi == 0)
    def _():
        for which in range(2):
            bkt = bkt_ref[which]
            for h in range(HEADS_PER_GROUP):
                t = jnp.zeros((BLK, BLK), F32)
                for bu in range(NUM_BUCKETS):
                    t = jnp.where(bkt == bu, bias_ref[bu, gi * HEADS_PER_GROUP + h], t)
                btile_ref[which, h] = t

    kbar = jnp.concatenate(
        [jnp.sum(k_ref[n * BLK:(n + 1) * BLK, :].astype(F32), axis=0, keepdims=True) for n in range(n_blk)],
        axis=0) * (1.0 / BLK)
    cmax_ref[...] = jnp.full(cmax_ref.shape, NEG, F32)
    lane_head = lax.broadcasted_iota(jnp.int32, kbar.shape, 1) // HEAD_DIM
    pieces, rem = [], kbar
    for _ in range(3):
        p = rem.astype(BF16).astype(F32)
        pieces.append(p)
        rem = rem - p
    rows = [jnp.where(lane_head == h, p, 0.0) for p in pieces for h in range(HEADS_PER_GROUP)]
    r = jnp.concatenate(rows, axis=0).astype(BF16)
    gt = lax.dot_general(r, q_ref[...], NT_DIMS, preferred_element_type=F32)
    npc = HEADS_PER_GROUP * n_blk
    gate = gt[0:npc] + gt[npc:2 * npc] + gt[2 * npc:3 * npc]
    blk_idx = lax.broadcasted_iota(jnp.int32, (n_blk, s_len), 0)
    cur_blk = lax.broadcasted_iota(jnp.int32, (n_blk, s_len), 1) // BLK
    past = blk_idx < cur_blk
    for h in range(HEADS_PER_GROUP):
        g = jnp.where(past, gate[h * n_blk:(h + 1) * n_blk], -jnp.inf)
        beaten_by = jnp.zeros((n_blk, s_len), jnp.int32)
        for m in range(n_blk):
            gm = g[m:m + 1, :]
            beats = (gm > g) | ((gm == g) & (m < blk_idx))
            beaten_by = beaten_by + beats.astype(jnp.int32)
        selected = (beaten_by < MOBA_TOPK) & past
        madd_ref[h] = jnp.where(selected, 0.0, NEG)

    s_idx = lax.broadcasted_iota(jnp.int32, (BLK, BLK), 0)
    t_idx = lax.broadcasted_iota(jnp.int32, (BLK, BLK), 1)
    causal = s_idx <= t_idx

    def q_block(i, _):
        q0 = pl.multiple_of(i * BLK, BLK)
        qb = q_ref[pl.ds(q0, BLK), :]

        def logits(j, kind):
            kb = k_ref[pl.ds(pl.multiple_of(j * BLK, BLK), BLK), :]
            z = lax.dot_general(_head_stack_keys(kb), qb, NT_DIMS, preferred_element_type=F32)
            for h in range(HEADS_PER_GROUP):
                zh = z[h * BLK:(h + 1) * BLK]
                if kind == "own":
                    zh = jnp.where(causal, zh + btile_ref[0, h], NEG)
                elif kind == "prev":
                    zh = zh + btile_ref[1, h]
                logit_ref[j, h] = zh
                cmax_ref[h, pl.ds(j, 1), :] = jnp.max(zh, axis=0, keepdims=True)

        logits(i, "own")

        @pl.when(i >= 1)
        def _():
            logits(i - 1, "prev")

        def far_logits(j, _):
            logits(j, "far")
            return 0

        lax.fori_loop(0, i - 1, far_logits, 0)

        offs = []
        for h in range(HEADS_PER_GROUP):
            far_bias = bias_ref[far_bucket, gi * HEADS_PER_GROUP + h]
            madd = madd_ref[h, :, pl.ds(q0, BLK)]
            cm = cmax_ref[h]
            jrow = lax.broadcasted_iota(jnp.int32, cm.shape, 0)
            cand = jnp.where(jrow < i - 1, cm + far_bias + madd,
                             jnp.where(jrow == i - 1, cm + madd, jnp.where(jrow == i, cm, NEG)))
            mx = jnp.max(cand, axis=0, keepdims=True)
            offs.append((madd, mx, far_bias))

        acc_ref[...] = jnp.zeros_like(acc_ref)

        def probs(j, own, lsum):
            vtb = vt_ref[:, pl.ds(pl.multiple_of(j * BLK, BLK), BLK)]
            out = []
            for h in range(HEADS_PER_GROUP):
                madd, mx, far_bias = offs[h]
                if own:
                    off = -mx
                else:
                    jrow = lax.broadcasted_iota(jnp.int32, madd.shape, 0)
                    mrow = jnp.max(jnp.where(jrow == j, madd, NEG), axis=0, keepdims=True)
                    off = mrow - mx + jnp.where(j < i - 1, far_bias, 0.0)
                p = jnp.exp(logit_ref[j, h] + off)
                out.append(lsum[h] + jnp.sum(p, axis=0, keepdims=True))
                acc_ref[...] += jnp.dot(_head_rows(vtb, h), p.astype(BF16), preferred_element_type=F32)
            return tuple(out)

        lsum = probs(i, True, tuple(jnp.zeros((1, BLK), F32) for _ in range(HEADS_PER_GROUP)))
        lsum = lax.fori_loop(0, i, lambda j, l: probs(j, False, l), lsum)

        inv = jnp.concatenate(
            [jnp.broadcast_to(1.0 / lsum[h], (HEAD_DIM, BLK)) for h in range(HEADS_PER_GROUP)], axis=0)
        o_ref[pl.ds(q0, BLK), :] = (acc_ref[...] * inv).T.astype(BF16)
        return 0

    lax.fori_loop(0, n_blk, q_block, 0)


def _moba_attention(qk, vt, rel_bias, buckets):
    b, s, _ = qk.shape
    n_blk = s // BLK
    return pl.pallas_call(
        _moba_kernel,
        grid=(N_GROUPS, b),
        in_specs=[
            pl.BlockSpec(memory_space=pltpu.SMEM),
            pl.BlockSpec((None, s, GROUP_WIDTH), lambda gi, bi: (bi, 0, 2 * N_GROUPS + gi)),
            pl.BlockSpec((None, s, GROUP_WIDTH), lambda gi, bi: (bi, 0, 3 * N_GROUPS + gi)),
            pl.BlockSpec((None, GROUP_WIDTH, s), lambda gi, bi: (bi, N_GROUPS + gi, 0)),
            pl.BlockSpec((2, BLK, BLK), lambda gi, bi: (0, 0, 0)),
        ],
        out_specs=pl.BlockSpec((None, s, GROUP_WIDTH), lambda gi, bi: (bi, 0, gi)),
        out_shape=jax.ShapeDtypeStruct((b, s, MIX_WIDTH), BF16),
        scratch_shapes=[
            pltpu.VMEM((2, HEADS_PER_GROUP, BLK, BLK), F32),
            pltpu.VMEM((HEADS_PER_GROUP, n_blk, s), F32),
            pltpu.VMEM((n_blk, HEADS_PER_GROUP, BLK, BLK), F32),
            pltpu.VMEM((HEADS_PER_GROUP, n_blk, BLK), F32),
            pltpu.VMEM((GROUP_WIDTH, BLK), F32),
        ],
        compiler_params=pltpu.CompilerParams(
            dimension_semantics=("arbitrary", "arbitrary"), vmem_limit_bytes=VMEM_LIMIT_BYTES),
        name="moba_attn",
    )(rel_bias, qk, qk, vt, buckets)


def _out_ffn_kernel(x_ref, osb_ref, omb_ref, gate_ref, p_ref, wbs_ref, wbm_ref, wo_ref, lnf_ref,
                    wfg_ref, wfu_ref, wfd_ref, lnp_ref, wpg_ref, wpp_ref, fin_ref, o_ref, hid_ref):
    d = x_ref.shape[1]
    y_sb = jnp.dot(osb_ref[...], wbs_ref[...], preferred_element_type=F32)
    y_mb = jnp.dot(omb_ref[...], wbm_ref[...], preferred_element_type=F32)
    mix = gate_ref[:, :d].astype(F32) * y_sb + gate_ref[:, d:].astype(F32) * y_mb
    x1 = x_ref[...] + jnp.dot(mix.astype(BF16), wo_ref[...], preferred_element_type=F32)

    h2 = _rms(x1, lnf_ref[...]).astype(BF16)
    ffn = wfg_ref.shape[1]
    chunk = 256
    for c in range(0, ffn, chunk):
        a = jnp.dot(h2, wfg_ref[:, c:c + chunk], preferred_element_type=F32)
        u = jnp.dot(h2, wfu_ref[:, c:c + chunk], preferred_element_type=F32)
        hid_ref[:, c:c + chunk] = (a * jax.nn.sigmoid(a) * u).astype(BF16)
    x2 = x1 + jnp.dot(hid_ref[...], wfd_ref[...], preferred_element_type=F32)

    h3 = _rms(x2, lnp_ref[...]).astype(BF16)
    g_ple = jax.nn.sigmoid(jnp.dot(h3, wpg_ref[...], preferred_element_type=F32))
    ple = jnp.dot(p_ref[...].astype(BF16), wpp_ref[...], preferred_element_type=F32)
    x3 = x2 + g_ple * ple
    o_ref[...] = _rms(x3, fin_ref[...])


def _out_ffn(x, o_sb, o_mb, gates, p, w_bs, w_bm, w_out, ln_ffn, w_fg, w_fu, w_fd, ln_ple, w_pg, w_pp,
             fin_g, tm):
    n, d = x.shape
    row = lambda width: pl.BlockSpec((tm, width), lambda i: (i, 0))

    def const(arr):
        return pl.BlockSpec(arr.shape, lambda i: (0, 0), pipeline_mode=pl.Buffered(1))

    weights = (w_bs, w_bm, w_out, ln_ffn, w_fg, w_fu, w_fd, ln_ple, w_pg, w_pp, fin_g)
    return pl.pallas_call(
        _out_ffn_kernel,
        grid=(n // tm,),
        in_specs=[row(d), row(o_sb.shape[1]), row(o_mb.shape[1]), row(gates.shape[1]), row(p.shape[1])]
        + [const(w) for w in weights],
        out_specs=row(d),
        out_shape=jax.ShapeDtypeStruct((n, d), F32),
        scratch_shapes=[pltpu.VMEM((tm, w_fg.shape[1]), BF16)],
        compiler_params=pltpu.CompilerParams(
            dimension_semantics=("arbitrary",), vmem_limit_bytes=VMEM_LIMIT_BYTES),
        name="out_ffn",
    )(x, o_sb, o_mb, gates, p, *weights)


def _t5_bucket_table(n):
    dist = np.arange(n)
    nf = np.maximum(dist, 1).astype(np.float32)
    large = MAX_EXACT + (np.log(nf / np.float32(MAX_EXACT)) / np.float32(math.log(MAX_DISTANCE / MAX_EXACT))
                         * np.float32(NUM_BUCKETS - MAX_EXACT)).astype(np.int32)
    large = np.minimum(large, NUM_BUCKETS - 1)
    return np.where(dist < MAX_EXACT, dist, large).astype(np.int32)


def _bucket_tiles():
    table = _t5_bucket_table(2 * BLK)
    s_idx = np.arange(BLK)[:, None]
    t_idx = np.arange(BLK)[None, :]
    own = table[np.maximum(t_idx - s_idx, 0)]
    prev = table[t_idx - s_idx + BLK]
    return np.stack([own, prev]).astype(np.int32)


def _suffix_matrix():
    t = (np.arange(BLK)[None, :] > np.arange(BLK)[:, None]).astype(np.float32)
    return np.concatenate([t, t], axis=1)


def kernel(x, p, ln_mix_g, w_in, w_gate, b_gate, w_branch_sb, w_branch_moba, w_out, rel_bias, ln_ffn_g,
           w_ffn_gate, w_ffn_up, w_ffn_down, ln_ple_g, w_ple_gate, w_ple_proj, final_g):
    depth = w_in.shape[0]
    assert depth == 1, "the final RMSNorm is fused into the last layer's kernel"
    b, s, d = x.shape
    assert s % BLK == 0
    scale = HEAD_DIM ** -0.5
    tri = jnp.asarray(_suffix_matrix(), BF16)
    buckets = jnp.asarray(_bucket_tiles())
    tm_in = min(256, s)
    tm_out = min(256, b * s)
    for i in range(depth):
        w = w_in[i]
        q_sb, k_sb, v_sb, q_mb, k_mb, v_mb = (w[:, j * MIX_WIDTH:(j + 1) * MIX_WIDTH] for j in range(6))
        w_qk = jnp.concatenate([q_sb * scale, k_sb, q_mb * scale, k_mb], axis=1).astype(BF16)
        w_vt = jnp.concatenate([v_sb, v_mb], axis=1).T.astype(BF16)
        qk, vt, gates = _in_proj(x, ln_mix_g[i][None], w_qk, w_vt, w_gate[i].astype(BF16), b_gate[i][None], tm_in)
        o_sb = _sb_attention(qk, vt, tri)
        o_mb = _moba_attention(qk, vt, rel_bias, buckets)
        x = _out_ffn(
            x.reshape(b * s, d), o_sb.reshape(b * s, MIX_WIDTH), o_mb.reshape(b * s, MIX_WIDTH),
            gates.reshape(b * s, 2 * d), p[i].reshape(b * s, -1),
            w_branch_sb[i].astype(BF16), w_branch_moba[i].astype(BF16), w_out[i].astype(BF16),
            ln_ffn_g[i][None], w_ffn_gate[i].astype(BF16), w_ffn_up[i].astype(BF16),
            w_ffn_down[i].astype(BF16), ln_ple_g[i][None], w_ple_gate[i].astype(BF16),
            w_ple_proj[i].astype(BF16), final_g[None], tm_out).reshape(b, s, d)
    return x
```

```python
import functools
import math

import jax
import jax.numpy as jnp
import numpy as np
from jax import lax
from jax.experimental import pallas as pl
from jax.experimental.pallas import tpu as pltpu

HEAD_DIM = 64
N_HEADS = 8
MIX_WIDTH = N_HEADS * HEAD_DIM
HEADS_PER_GROUP = 4
GROUP_WIDTH = HEADS_PER_GROUP * HEAD_DIM
N_GROUPS = N_HEADS // HEADS_PER_GROUP
BLK = 256
MOBA_TOPK = 3
NUM_BUCKETS = 32
MAX_EXACT = NUM_BUCKETS // 2
MAX_DISTANCE = 128
RMS_EPS = 1e-6
HALF = BLK // 2
LOG2_E = 1.0 / math.log(2.0)
SB_SKIP_BITS = 160.0
NEG = -1e30
VMEM_LIMIT_BYTES = 56 * 1024 * 1024

F32 = jnp.float32
BF16 = jnp.bfloat16
NT_DIMS = (((1,), (1,)), ((), ()))


def _rms(x, g):
    return x * lax.rsqrt(jnp.mean(x * x, axis=-1, keepdims=True) + RMS_EPS) * g


def _in_proj_kernel(x_ref, g_ref, wqk_ref, wvt_ref, wg_ref, bg_ref, qk_ref, vt_ref, gate_ref):
    h = _rms(x_ref[...], g_ref[...]).astype(BF16)
    qk_ref[...] = jnp.dot(h, wqk_ref[...], preferred_element_type=F32).astype(BF16)
    vt_ref[...] = lax.dot_general(wvt_ref[...], h, NT_DIMS, preferred_element_type=F32).astype(BF16)
    gl = jnp.dot(h, wg_ref[...], preferred_element_type=F32) + bg_ref[...]
    gate_ref[...] = jax.nn.sigmoid(gl).astype(BF16)


def _in_proj(x, ln_g, w_qk, w_vt, w_gate, b_gate, tm):
    b, s, d = x.shape
    n_qk, n_v, n_g = w_qk.shape[1], w_vt.shape[0], w_gate.shape[1]

    def const(shape):
        return pl.BlockSpec(shape, lambda bi, ti: (0, 0), pipeline_mode=pl.Buffered(1))

    return pl.pallas_call(
        _in_proj_kernel,
        grid=(b, s // tm),
        in_specs=[
            pl.BlockSpec((None, tm, d), lambda bi, ti: (bi, ti, 0)),
            const((1, d)),
            const((d, n_qk)),
            const((n_v, d)),
            const((d, n_g)),
            const((1, n_g)),
        ],
        out_specs=[
            pl.BlockSpec((None, tm, n_qk), lambda bi, ti: (bi, ti, 0)),
            pl.BlockSpec((None, n_v, tm), lambda bi, ti: (bi, 0, ti)),
            pl.BlockSpec((None, tm, n_g), lambda bi, ti: (bi, ti, 0)),
        ],
        out_shape=[
            jax.ShapeDtypeStruct((b, s, n_qk), BF16),
            jax.ShapeDtypeStruct((b, n_v, s), BF16),
            jax.ShapeDtypeStruct((b, s, n_g), BF16),
        ],
        compiler_params=pltpu.CompilerParams(
            dimension_semantics=("arbitrary", "arbitrary"), vmem_limit_bytes=VMEM_LIMIT_BYTES),
        name="in_proj",
    )(x, ln_g, w_qk, w_vt, w_gate, b_gate)


def _sb_kernel(q_ref, k_ref, vt_ref, tri_ref, o_ref, acc_ref, qm_ref):
    n_blk = q_ref.shape[0] // BLK
    tri = tri_ref[...]
    s_idx = lax.broadcasted_iota(jnp.int32, (BLK, BLK), 0)
    t_idx = lax.broadcasted_iota(jnp.int32, (BLK, BLK), 1)
    causal = s_idx < t_idx
    lane_head = lax.broadcasted_iota(jnp.int32, (BLK, GROUP_WIDTH), 1) // HEAD_DIM

    def tile(j, carry, diagonal):
        k0 = pl.multiple_of(j * BLK, BLK)
        kb = k_ref[pl.ds(k0, BLK), :]
        heads = range(HEADS_PER_GROUP)
        z = [lax.dot_general(kb, qm_ref[h], NT_DIMS, preferred_element_type=F32) for h in heads]
        hi, lo = [], []
        for h in heads:
            a = jnp.maximum(z[h], 0.0) + jnp.log(1.0 + jnp.exp2(-jnp.abs(z[h]))) * LOG2_E
            if diagonal:
                a = jnp.where(causal, a, 0.0)
            hi.append(a.astype(BF16))
            lo.append((a - hi[h].astype(F32)).astype(BF16))
        inc_far = [jnp.dot(tri, jnp.concatenate([hi[h][HALF:], lo[h][HALF:]], axis=0),
                           preferred_element_type=F32) for h in heads]
        inc_near = [jnp.dot(tri, jnp.concatenate([hi[h][:HALF], lo[h][:HALF]], axis=0),
                            preferred_element_type=F32) for h in heads]
        new_carry = []
        for h in heads:
            row_far = carry[h]
            row_near = row_far + inc_far[h][0:1, :]
            arg = jnp.concatenate([z[h][:HALF] - inc_near[h] - row_near, z[h][HALF:] - inc_far[h] - row_far],
                                  axis=0)
            w = jnp.exp2(arg)
            if diagonal:
                w = jnp.where(causal, w, 0.0)
            rows = slice(h * HEAD_DIM, (h + 1) * HEAD_DIM)
            acc_ref[rows, :] += jnp.dot(vt_ref[rows, pl.ds(k0, BLK)], w.astype(BF16),
                                        preferred_element_type=F32)
            new_carry.append(row_near + inc_near[h][0:1, :])
        return tuple(new_carry)

    def live(carry):
        return jnp.min(jnp.concatenate(carry, axis=0)) < SB_SKIP_BITS

    def q_block(i, _):
        q0 = pl.multiple_of(i * BLK, BLK)
        qb = q_ref[pl.ds(q0, BLK), :]
        for h in range(HEADS_PER_GROUP):
            qm_ref[h] = jnp.where(lane_head == h, qb, jnp.zeros_like(qb))
        acc_ref[...] = jnp.zeros_like(acc_ref)
        carry = tile(i, tuple(jnp.zeros((1, BLK), F32) for _ in range(HEADS_PER_GROUP)), True)

        def cond(state):
            jj, go, _ = state
            return (jj < i) & go

        def body(state):
            jj, _, c = state
            c = tile(i - 1 - jj, c, False)
            return jj + 1, live(c), c

        lax.while_loop(cond, body, (0, live(carry), carry))
        o_ref[pl.ds(q0, BLK), :] = acc_ref[...].T.astype(BF16)
        return 0

    lax.fori_loop(0, n_blk, q_block, 0)


def _sb_attention(qk, vt, tri):
    b, s, _ = qk.shape
    return pl.pallas_call(
        _sb_kernel,
        grid=(b, N_GROUPS),
        in_specs=[
            pl.BlockSpec((None, s, GROUP_WIDTH), lambda bi, gi: (bi, 0, gi)),
            pl.BlockSpec((None, s, GROUP_WIDTH), lambda bi, gi: (bi, 0, N_GROUPS + gi)),
            pl.BlockSpec((None, GROUP_WIDTH, s), lambda bi, gi: (bi, gi, 0)),
            pl.BlockSpec((HALF, 2 * HALF), lambda bi, gi: (0, 0)),
        ],
        out_specs=pl.BlockSpec((None, s, GROUP_WIDTH), lambda bi, gi: (bi, 0, gi)),
        out_shape=jax.ShapeDtypeStruct((b, s, MIX_WIDTH), BF16),
        scratch_shapes=[
            pltpu.VMEM((GROUP_WIDTH, BLK), F32),
            pltpu.VMEM((HEADS_PER_GROUP, BLK, GROUP_WIDTH), BF16),
        ],
        compiler_params=pltpu.CompilerParams(
            dimension_semantics=("arbitrary", "arbitrary"), vmem_limit_bytes=VMEM_LIMIT_BYTES),
        name="sb_attn",
    )(qk, qk, vt, tri)


def _moba_kernel(bias_ref, q_ref, k_ref, vt_ref, bkt_ref, o_ref, btile_ref, madd_ref):
    gi = pl.program_id(0)
    bi = pl.program_id(1)
    s_len = q_ref.shape[0]
    n_blk = s_len // BLK
    far_bucket = NUM_BUCKETS - 1

    @pl.when(bi == 0)
    def _():
        for which in range(2):
            bkt = bkt_ref[which]
            for h in range(HEADS_PER_GROUP):
                t = jnp.zeros((BLK, BLK), F32)
                for bu in range(NUM_BUCKETS):
                    t = jnp.where(bkt == bu, bias_ref[bu, gi * HEADS_PER_GROUP + h] * LOG2_E, t)
                btile_ref[which, h] = t

    kbar = jnp.concatenate(
        [jnp.sum(k_ref[n * BLK:(n + 1) * BLK, :].astype(F32), axis=0, keepdims=True) for n in range(n_blk)],
        axis=0) * (1.0 / BLK)
    lane_head = lax.broadcasted_iota(jnp.int32, kbar.shape, 1) // HEAD_DIM
    pieces, rem = [], kbar
    for _ in range(3):
        p = rem.astype(BF16).astype(F32)
        pieces.append(p)
        rem = rem - p
    rows = [jnp.where(lane_head == h, p, 0.0) for p in pieces for h in range(HEADS_PER_GROUP)]
    r = jnp.concatenate(rows, axis=0).astype(BF16)
    gt = lax.dot_general(r, q_ref[...], NT_DIMS, preferred_element_type=F32)
    npc = HEADS_PER_GROUP * n_blk
    gate = gt[0:npc] + gt[npc:2 * npc] + gt[2 * npc:3 * npc]
    blk_idx = lax.broadcasted_iota(jnp.int32, (n_blk, s_len), 0)
    cur_blk = lax.broadcasted_iota(jnp.int32, (n_blk, s_len), 1) // BLK
    past = blk_idx < cur_blk
    for h in range(HEADS_PER_GROUP):
        g = jnp.where(past, gate[h * n_blk:(h + 1) * n_blk], -jnp.inf)
        beaten_by = jnp.zeros((n_blk, s_len), jnp.int32)
        for m in range(n_blk):
            gm = g[m:m + 1, :]
            beats = (gm > g) | ((gm == g) & (m < blk_idx))
            beaten_by = beaten_by + beats.astype(jnp.int32)
        selected = (beaten_by < MOBA_TOPK) & past
        madd_ref[h] = jnp.where(selected, 0.0, NEG)

    s_idx = lax.broadcasted_iota(jnp.int32, (BLK, BLK), 0)
    t_idx = lax.broadcasted_iota(jnp.int32, (BLK, BLK), 1)
    causal = s_idx <= t_idx
    lane_head_q = lax.broadcasted_iota(jnp.int32, (BLK, GROUP_WIDTH), 1) // HEAD_DIM

    heads = range(HEADS_PER_GROUP)
    far_bias = [bias_ref[far_bucket, gi * HEADS_PER_GROUP + h] * LOG2_E for h in heads]

    for i in range(n_blk):
        q_lo, n_keys = i * BLK, (i + 1) * BLK
        qb = q_ref[q_lo:q_lo + BLK, :]
        keys = k_ref[0:n_keys, :]
        z = [lax.dot_general(keys, jnp.where(lane_head_q == h, qb, jnp.zeros_like(qb)), NT_DIMS,
                             preferred_element_type=F32) for h in heads]
        out_rows = []
        for h in heads:
            madd = madd_ref[h, :, q_lo:q_lo + BLK]
            blocks, adds = [], []
            for j in range(i + 1):
                zj = z[h][j * BLK:(j + 1) * BLK]
                if j == i:
                    blocks.append(jnp.where(causal, zj + btile_ref[0, h], NEG))
                    adds.append(None)
                elif j == i - 1:
                    blocks.append(zj + btile_ref[1, h])
                    adds.append(madd[j:j + 1, :])
                else:
                    blocks.append(zj)
                    adds.append(madd[j:j + 1, :] + far_bias[h])
            mx = None
            for blk, add in zip(blocks, adds):
                cm = jnp.max(blk, axis=0, keepdims=True)
                cm = cm if add is None else cm + add
                mx = cm if mx is None else jnp.maximum(mx, cm)
            lsum, probs = None, []
            for blk, add in zip(blocks, adds):
                p = jnp.exp2(blk + ((-mx) if add is None else (add - mx)))
                ps = jnp.sum(p, axis=0, keepdims=True)
                lsum = ps if lsum is None else lsum + ps
                probs.append(p.astype(BF16))
            rows = slice(h * HEAD_DIM, (h + 1) * HEAD_DIM)
            pv = jnp.dot(vt_ref[rows, 0:n_keys], jnp.concatenate(probs, axis=0), preferred_element_type=F32)
            out_rows.append(pv * (1.0 / lsum))
        o_ref[q_lo:q_lo + BLK, :] = jnp.concatenate(out_rows, axis=0).T.astype(BF16)


def _moba_attention(qk, vt, rel_bias, buckets):
    b, s, _ = qk.shape
    n_blk = s // BLK
    return pl.pallas_call(
        _moba_kernel,
        grid=(N_GROUPS, b),
        in_specs=[
            pl.BlockSpec(memory_space=pltpu.SMEM),
            pl.BlockSpec((None, s, GROUP_WIDTH), lambda gi, bi: (bi, 0, 2 * N_GROUPS + gi)),
            pl.BlockSpec((None, s, GROUP_WIDTH), lambda gi, bi: (bi, 0, 3 * N_GROUPS + gi)),
            pl.BlockSpec((None, GROUP_WIDTH, s), lambda gi, bi: (bi, N_GROUPS + gi, 0)),
            pl.BlockSpec((2, BLK, BLK), lambda gi, bi: (0, 0, 0)),
        ],
        out_specs=pl.BlockSpec((None, s, GROUP_WIDTH), lambda gi, bi: (bi, 0, gi)),
        out_shape=jax.ShapeDtypeStruct((b, s, MIX_WIDTH), BF16),
        scratch_shapes=[
            pltpu.VMEM((2, HEADS_PER_GROUP, BLK, BLK), F32),
            pltpu.VMEM((HEADS_PER_GROUP, n_blk, s), F32),
        ],
        compiler_params=pltpu.CompilerParams(
            dimension_semantics=("arbitrary", "arbitrary"), vmem_limit_bytes=VMEM_LIMIT_BYTES),
        name="moba_attn",
    )(rel_bias, qk, qk, vt, buckets)


def _out_ffn_kernel(x_ref, osb_ref, omb_ref, gate_ref, p_ref, wbs_ref, wbm_ref, wo_ref, lnf_ref,
                    wfg_ref, wfu_ref, wfd_ref, lnp_ref, wpg_ref, wpp_ref, fin_ref, o_ref, hid_ref):
    d = x_ref.shape[1]
    y_sb = jnp.dot(osb_ref[...], wbs_ref[...], preferred_element_type=F32)
    y_mb = jnp.dot(omb_ref[...], wbm_ref[...], preferred_element_type=F32)
    mix = gate_ref[:, :d].astype(F32) * y_sb + gate_ref[:, d:].astype(F32) * y_mb
    x1 = x_ref[...] + jnp.dot(mix.astype(BF16), wo_ref[...], preferred_element_type=F32)

    h2 = _rms(x1, lnf_ref[...]).astype(BF16)
    ffn = wfg_ref.shape[1]
    chunk = 256
    for c in range(0, ffn, chunk):
        a = jnp.dot(h2, wfg_ref[:, c:c + chunk], preferred_element_type=F32)
        u = jnp.dot(h2, wfu_ref[:, c:c + chunk], preferred_element_type=F32)
        hid_ref[:, c:c + chunk] = (a * jax.nn.sigmoid(a) * u).astype(BF16)
    x2 = x1 + jnp.dot(hid_ref[...], wfd_ref[...], preferred_element_type=F32)

    h3 = _rms(x2, lnp_ref[...]).astype(BF16)
    g_ple = jax.nn.sigmoid(jnp.dot(h3, wpg_ref[...], preferred_element_type=F32))
    ple = jnp.dot(p_ref[...].astype(BF16), wpp_ref[...], preferred_element_type=F32)
    x3 = x2 + g_ple * ple
    o_ref[...] = _rms(x3, fin_ref[...])


def _out_ffn(x, o_sb, o_mb, gates, p, w_bs, w_bm, w_out, ln_ffn, w_fg, w_fu, w_fd, ln_ple, w_pg, w_pp,
             fin_g, tm):
    n, d = x.shape
    row = lambda width: pl.BlockSpec((tm, width), lambda i: (i, 0))

    def const(arr):
        return pl.BlockSpec(arr.shape, lambda i: (0, 0), pipeline_mode=pl.Buffered(1))

    weights = (w_bs, w_bm, w_out, ln_ffn, w_fg, w_fu, w_fd, ln_ple, w_pg, w_pp, fin_g)
    return pl.pallas_call(
        _out_ffn_kernel,
        grid=(n // tm,),
        in_specs=[row(d), row(o_sb.shape[1]), row(o_mb.shape[1]), row(gates.shape[1]), row(p.shape[1])]
        + [const(w) for w in weights],
        out_specs=row(d),
        out_shape=jax.ShapeDtypeStruct((n, d), F32),
        scratch_shapes=[pltpu.VMEM((tm, w_fg.shape[1]), BF16)],
        compiler_params=pltpu.CompilerParams(
            dimension_semantics=("arbitrary",), vmem_limit_bytes=VMEM_LIMIT_BYTES),
        name="out_ffn",
    )(x, o_sb, o_mb, gates, p, *weights)


def _t5_bucket_table(n):
    dist = np.arange(n)
    nf = np.maximum(dist, 1).astype(np.float32)
    large = MAX_EXACT + (np.log(nf / np.float32(MAX_EXACT)) / np.float32(math.log(MAX_DISTANCE / MAX_EXACT))
                         * np.float32(NUM_BUCKETS - MAX_EXACT)).astype(np.int32)
    large = np.minimum(large, NUM_BUCKETS - 1)
    return np.where(dist < MAX_EXACT, dist, large).astype(np.int32)


def _bucket_tiles():
    table = _t5_bucket_table(2 * BLK)
    s_idx = np.arange(BLK)[:, None]
    t_idx = np.arange(BLK)[None, :]
    own = table[np.maximum(t_idx - s_idx, 0)]
    prev = table[t_idx - s_idx + BLK]
    return np.stack([own, prev]).astype(np.int32)


def _suffix_matrix():
    t = (np.arange(HALF)[None, :] >= np.arange(HALF)[:, None]).astype(np.float32)
    return np.concatenate([t, t], axis=1)


def kernel(x, p, ln_mix_g, w_in, w_gate, b_gate, w_branch_sb, w_branch_moba, w_out, rel_bias, ln_ffn_g,
           w_ffn_gate, w_ffn_up, w_ffn_down, ln_ple_g, w_ple_gate, w_ple_proj, final_g):
    depth = w_in.shape[0]
    assert depth == 1, "the final RMSNorm is fused into the last layer's kernel"
    b, s, d = x.shape
    assert s % BLK == 0
    scale = HEAD_DIM ** -0.5
    sb_scale = scale * LOG2_E
    tri = jnp.asarray(_suffix_matrix(), BF16)
    buckets = jnp.asarray(_bucket_tiles())
    tm_in = min(256, s)
    tm_out = min(256, b * s)
    for i in range(depth):
        w = w_in[i]
        q_sb, k_sb, v_sb, q_mb, k_mb, v_mb = (w[:, j * MIX_WIDTH:(j + 1) * MIX_WIDTH] for j in range(6))
        w_qk = jnp.concatenate([q_sb * sb_scale, k_sb, q_mb * sb_scale, k_mb], axis=1).astype(BF16)
        w_vt = jnp.concatenate([v_sb, v_mb], axis=1).T.astype(BF16)
        qk, vt, gates = _in_proj(x, ln_mix_g[i][None], w_qk, w_vt, w_gate[i].astype(BF16), b_gate[i][None], tm_in)
        o_sb = _sb_attention(qk, vt, tri)
        o_mb = _moba_attention(qk, vt, rel_bias, buckets)
        x = _out_ffn(
            x.reshape(b * s, d), o_sb.reshape(b * s, MIX_WIDTH), o_mb.reshape(b * s, MIX_WIDTH),
            gates.reshape(b * s, 2 * d), p[i].reshape(b * s, -1),
            w_branch_sb[i].astype(BF16), w_branch_moba[i].astype(BF16), w_out[i].astype(BF16),
            ln_ffn_g[i][None], w_ffn_gate[i].astype(BF16), w_ffn_up[i].astype(BF16),
            w_ffn_down[i].astype(BF16), ln_ple_g[i][None], w_ple_gate[i].astype(BF16),
            w_ple_proj[i].astype(BF16), final_g[None], tm_out).reshape(b, s, d)
    return x
```

```python
import functools
import math

import jax
import jax.numpy as jnp
import numpy as np
from jax import lax
from jax.experimental import pallas as pl
from jax.experimental.pallas import tpu as pltpu

HEAD_DIM = 64
N_HEADS = 8
MIX_WIDTH = N_HEADS * HEAD_DIM
HEADS_PER_GROUP = 4
GROUP_WIDTH = HEADS_PER_GROUP * HEAD_DIM
N_GROUPS = N_HEADS // HEADS_PER_GROUP
BLK = 256
MOBA_TOPK = 3
NUM_BUCKETS = 32
MAX_EXACT = NUM_BUCKETS // 2
MAX_DISTANCE = 128
RMS_EPS = 1e-6
HALF = BLK // 2
LOG2_E = 1.0 / math.log(2.0)
SB_SKIP_BITS = 160.0
NEG = -1e30
VMEM_LIMIT_BYTES = 56 * 1024 * 1024

F32 = jnp.float32
BF16 = jnp.bfloat16
NT_DIMS = (((1,), (1,)), ((), ()))


def _rms(x, g):
    return x * lax.rsqrt(jnp.mean(x * x, axis=-1, keepdims=True) + RMS_EPS) * g


def _in_proj_kernel(x_ref, g_ref, wqk_ref, wvt_ref, wg_ref, bg_ref, qk_ref, vt_ref, gate_ref):
    h = _rms(x_ref[...], g_ref[...]).astype(BF16)
    qk_ref[...] = jnp.dot(h, wqk_ref[...], preferred_element_type=F32).astype(BF16)
    vt_ref[...] = lax.dot_general(wvt_ref[...], h, NT_DIMS, preferred_element_type=F32).astype(BF16)
    gl = jnp.dot(h, wg_ref[...], preferred_element_type=F32) + bg_ref[...]
    gate_ref[...] = jax.nn.sigmoid(gl).astype(BF16)


def _in_proj(x, ln_g, w_qk, w_vt, w_gate, b_gate, tm):
    b, s, d = x.shape
    n_qk, n_v, n_g = w_qk.shape[1], w_vt.shape[0], w_gate.shape[1]

    def const(shape):
        return pl.BlockSpec(shape, lambda bi, ti: (0, 0), pipeline_mode=pl.Buffered(1))

    return pl.pallas_call(
        _in_proj_kernel,
        grid=(b, s // tm),
        in_specs=[
            pl.BlockSpec((None, tm, d), lambda bi, ti: (bi, ti, 0)),
            const((1, d)),
            const((d, n_qk)),
            const((n_v, d)),
            const((d, n_g)),
            const((1, n_g)),
        ],
        out_specs=[
            pl.BlockSpec((None, tm, n_qk), lambda bi, ti: (bi, ti, 0)),
            pl.BlockSpec((None, n_v, tm), lambda bi, ti: (bi, 0, ti)),
            pl.BlockSpec((None, tm, n_g), lambda bi, ti: (bi, ti, 0)),
        ],
        out_shape=[
            jax.ShapeDtypeStruct((b, s, n_qk), BF16),
            jax.ShapeDtypeStruct((b, n_v, s), BF16),
            jax.ShapeDtypeStruct((b, s, n_g), BF16),
        ],
        compiler_params=pltpu.CompilerParams(
            dimension_semantics=("arbitrary", "arbitrary"), vmem_limit_bytes=VMEM_LIMIT_BYTES),
        name="in_proj",
    )(x, ln_g, w_qk, w_vt, w_gate, b_gate)


def _sb_kernel(q_ref, k_ref, vt_ref, tri_ref, o_ref, acc_ref, qm_ref):
    n_blk = q_ref.shape[0] // BLK
    tri = tri_ref[...]
    lane_head = lax.broadcasted_iota(jnp.int32, (BLK, GROUP_WIDTH), 1) // HEAD_DIM
    heads = range(HEADS_PER_GROUP)
    aligned = lambda x: x if isinstance(x, int) else pl.multiple_of(x, BLK)

    def tile(k0, n_kb, carry, diagonal):
        n_keys, n_half = n_kb * BLK, 2 * n_kb
        keys = k_ref[pl.ds(k0, n_keys), :]

        def mask_diagonal(x):
            s_idx = lax.broadcasted_iota(jnp.int32, (BLK, BLK), 0)
            t_idx = lax.broadcasted_iota(jnp.int32, (BLK, BLK), 1)
            last = jnp.where(s_idx < t_idx, x[n_keys - BLK:], 0.0)
            return last if n_kb == 1 else jnp.concatenate([x[:n_keys - BLK], last], axis=0)

        z = [lax.dot_general(keys, qm_ref[h], NT_DIMS, preferred_element_type=F32) for h in heads]
        hi, lo = [], []
        for h in heads:
            a = jnp.maximum(z[h], 0.0) + jnp.log(1.0 + jnp.exp2(-jnp.abs(z[h]))) * LOG2_E
            if diagonal:
                a = mask_diagonal(a)
            hi.append(a.astype(BF16))
            lo.append((a - hi[h].astype(F32)).astype(BF16))
        half = lambda x, c: x[c * HALF:(c + 1) * HALF]
        inc = [[jnp.dot(tri, jnp.concatenate([half(hi[h], c), half(lo[h], c)], axis=0),
                        preferred_element_type=F32) for c in range(n_half)] for h in heads]
        new_carry = []
        for h in heads:
            row, args = carry[h], [None] * n_half
            for c in reversed(range(n_half)):
                args[c] = half(z[h], c) - inc[h][c] - row
                row = row + inc[h][c][0:1, :]
            w = jnp.exp2(jnp.concatenate(args, axis=0))
            if diagonal:
                w = mask_diagonal(w)
            rows = slice(h * HEAD_DIM, (h + 1) * HEAD_DIM)
            pv = jnp.dot(vt_ref[rows, pl.ds(k0, n_keys)], w.astype(BF16), preferred_element_type=F32)
            if diagonal:
                acc_ref[rows, :] = pv
            else:
                acc_ref[rows, :] += pv
            new_carry.append(row)
        return tuple(new_carry)

    def live(carry):
        return jnp.min(jnp.concatenate(carry, axis=0)) < SB_SKIP_BITS

    def q_block(i, n_near):
        q0 = aligned(i * BLK)
        qb = q_ref[pl.ds(q0, BLK), :]
        for h in heads:
            qm_ref[h] = jnp.where(lane_head == h, qb, jnp.zeros_like(qb))
        first = i - (n_near - 1)
        carry = tile(aligned(first * BLK), n_near,
                     tuple(jnp.zeros((1, BLK), F32) for _ in heads), True)

        def cond(state):
            jj, go, _ = state
            return (jj < first) & go

        def body(state):
            jj, _, c = state
            c = tile(aligned((first - 1 - jj) * BLK), 1, c, False)
            return jj + 1, live(c), c

        lax.while_loop(cond, body, (0, live(carry), carry))
        o_ref[pl.ds(q0, BLK), :] = acc_ref[...].T.astype(BF16)

    q_block(0, 1)

    def later_block(i, _):
        q_block(i, 2)
        return 0

    lax.fori_loop(1, n_blk, later_block, 0)


def _sb_attention(qk, vt, tri):
    b, s, _ = qk.shape
    return pl.pallas_call(
        _sb_kernel,
        grid=(b, N_GROUPS),
        in_specs=[
            pl.BlockSpec((None, s, GROUP_WIDTH), lambda bi, gi: (bi, 0, gi)),
            pl.BlockSpec((None, s, GROUP_WIDTH), lambda bi, gi: (bi, 0, N_GROUPS + gi)),
            pl.BlockSpec((None, GROUP_WIDTH, s), lambda bi, gi: (bi, gi, 0)),
            pl.BlockSpec((HALF, 2 * HALF), lambda bi, gi: (0, 0)),
        ],
        out_specs=pl.BlockSpec((None, s, GROUP_WIDTH), lambda bi, gi: (bi, 0, gi)),
        out_shape=jax.ShapeDtypeStruct((b, s, MIX_WIDTH), BF16),
        scratch_shapes=[
            pltpu.VMEM((GROUP_WIDTH, BLK), F32),
            pltpu.VMEM((HEADS_PER_GROUP, BLK, GROUP_WIDTH), BF16),
        ],
        compiler_params=pltpu.CompilerParams(
            dimension_semantics=("arbitrary", "arbitrary"), vmem_limit_bytes=VMEM_LIMIT_BYTES),
        name="sb_attn",
    )(qk, qk, vt, tri)


def _moba_kernel(bias_ref, q_ref, k_ref, vt_ref, bkt_ref, o_ref, btile_ref, madd_ref):
    gi = pl.program_id(0)
    bi = pl.program_id(1)
    s_len = q_ref.shape[0]
    n_blk = s_len // BLK
    far_bucket = NUM_BUCKETS - 1

    @pl.when(bi == 0)
    def _():
        for which in range(2):
            bkt = bkt_ref[which]
            for h in range(HEADS_PER_GROUP):
                t = jnp.zeros((BLK, BLK), F32)
                for bu in range(NUM_BUCKETS):
                    t = jnp.where(bkt == bu, bias_ref[bu, gi * HEADS_PER_GROUP + h] * LOG2_E, t)
                btile_ref[which, h] = t

    kbar = jnp.concatenate(
        [jnp.sum(k_ref[n * BLK:(n + 1) * BLK, :].astype(F32), axis=0, keepdims=True) for n in range(n_blk)],
        axis=0) * (1.0 / BLK)
    lane_head = lax.broadcasted_iota(jnp.int32, kbar.shape, 1) // HEAD_DIM
    pieces, rem = [], kbar
    for _ in range(3):
        p = rem.astype(BF16).astype(F32)
        pieces.append(p)
        rem = rem - p
    rows = [jnp.where(lane_head == h, p, 0.0) for p in pieces for h in range(HEADS_PER_GROUP)]
    r = jnp.concatenate(rows, axis=0).astype(BF16)
    gt = lax.dot_general(r, q_ref[...], NT_DIMS, preferred_element_type=F32)
    npc = HEADS_PER_GROUP * n_blk
    gate = gt[0:npc] + gt[npc:2 * npc] + gt[2 * npc:3 * npc]
    blk_idx = lax.broadcasted_iota(jnp.int32, (n_blk, s_len), 0)
    cur_blk = lax.broadcasted_iota(jnp.int32, (n_blk, s_len), 1) // BLK
    past = blk_idx < cur_blk
    for h in range(HEADS_PER_GROUP):
        g = jnp.where(past, gate[h * n_blk:(h + 1) * n_blk], -jnp.inf)
        beaten_by = jnp.zeros((n_blk, s_len), jnp.int32)
        for m in range(n_blk):
            gm = g[m:m + 1, :]
            beats = (gm > g) | ((gm == g) & (m < blk_idx))
            beaten_by = beaten_by + beats.astype(jnp.int32)
        selected = (beaten_by < MOBA_TOPK) & past
        madd_ref[h] = jnp.where(selected, 0.0, NEG)

    s_idx = lax.broadcasted_iota(jnp.int32, (BLK, BLK), 0)
    t_idx = lax.broadcasted_iota(jnp.int32, (BLK, BLK), 1)
    causal = s_idx <= t_idx
    lane_head_q = lax.broadcasted_iota(jnp.int32, (BLK, GROUP_WIDTH), 1) // HEAD_DIM

    heads = range(HEADS_PER_GROUP)
    far_bias = [bias_ref[far_bucket, gi * HEADS_PER_GROUP + h] * LOG2_E for h in heads]

    for i in range(n_blk):
        q_lo, n_keys = i * BLK, (i + 1) * BLK
        qb = q_ref[q_lo:q_lo + BLK, :]
        keys = k_ref[0:n_keys, :]
        z = [lax.dot_general(keys, jnp.where(lane_head_q == h, qb, jnp.zeros_like(qb)), NT_DIMS,
                             preferred_element_type=F32) for h in heads]
        out_rows = []
        for h in heads:
            madd = madd_ref[h, :, q_lo:q_lo + BLK]
            blocks, adds = [], []
            for j in range(i + 1):
                zj = z[h][j * BLK:(j + 1) * BLK]
                if j == i:
                    blocks.append(jnp.where(causal, zj + btile_ref[0, h], NEG))
                    adds.append(None)
                elif j == i - 1:
                    blocks.append(zj + btile_ref[1, h])
                    adds.append(madd[j:j + 1, :])
                else:
                    blocks.append(zj)
                    adds.append(madd[j:j + 1, :] + far_bias[h])
            mx = None
            for blk, add in zip(blocks, adds):
                cm = jnp.max(blk, axis=0, keepdims=True)
                cm = cm if add is None else cm + add
                mx = cm if mx is None else jnp.maximum(mx, cm)
            lsum, probs = None, []
            for blk, add in zip(blocks, adds):
                p = jnp.exp2(blk + ((-mx) if add is None else (add - mx)))
                ps = jnp.sum(p, axis=0, keepdims=True)
                lsum = ps if lsum is None else lsum + ps
                probs.append(p.astype(BF16))
            rows = slice(h * HEAD_DIM, (h + 1) * HEAD_DIM)
            pv = jnp.dot(vt_ref[rows, 0:n_keys], jnp.concatenate(probs, axis=0), preferred_element_type=F32)
            out_rows.append(pv * (1.0 / lsum))
        o_ref[q_lo:q_lo + BLK, :] = jnp.concatenate(out_rows, axis=0).T.astype(BF16)


def _moba_attention(qk, vt, rel_bias, buckets):
    b, s, _ = qk.shape
    n_blk = s // BLK
    return pl.pallas_call(
        _moba_kernel,
        grid=(N_GROUPS, b),
        in_specs=[
            pl.BlockSpec(memory_space=pltpu.SMEM),
            pl.BlockSpec((None, s, GROUP_WIDTH), lambda gi, bi: (bi, 0, 2 * N_GROUPS + gi)),
            pl.BlockSpec((None, s, GROUP_WIDTH), lambda gi, bi: (bi, 0, 3 * N_GROUPS + gi)),
            pl.BlockSpec((None, GROUP_WIDTH, s), lambda gi, bi: (bi, N_GROUPS + gi, 0)),
            pl.BlockSpec((2, BLK, BLK), lambda gi, bi: (0, 0, 0)),
        ],
        out_specs=pl.BlockSpec((None, s, GROUP_WIDTH), lambda gi, bi: (bi, 0, gi)),
        out_shape=jax.ShapeDtypeStruct((b, s, MIX_WIDTH), BF16),
        scratch_shapes=[
            pltpu.VMEM((2, HEADS_PER_GROUP, BLK, BLK), F32),
            pltpu.VMEM((HEADS_PER_GROUP, n_blk, s), F32),
        ],
        compiler_params=pltpu.CompilerParams(
            dimension_semantics=("arbitrary", "arbitrary"), vmem_limit_bytes=VMEM_LIMIT_BYTES),
        name="moba_attn",
    )(rel_bias, qk, qk, vt, buckets)


def _out_ffn_kernel(x_ref, osb_ref, omb_ref, gate_ref, p_ref, wbs_ref, wbm_ref, wo_ref, lnf_ref,
                    wfg_ref, wfu_ref, wfd_ref, lnp_ref, wpg_ref, wpp_ref, fin_ref, o_ref, hid_ref):
    d = x_ref.shape[1]
    y_sb = jnp.dot(osb_ref[...], wbs_ref[...], preferred_element_type=F32)
    y_mb = jnp.dot(omb_ref[...], wbm_ref[...], preferred_element_type=F32)
    mix = gate_ref[:, :d].astype(F32) * y_sb + gate_ref[:, d:].astype(F32) * y_mb
    x1 = x_ref[...] + jnp.dot(mix.astype(BF16), wo_ref[...], preferred_element_type=F32)

    h2 = _rms(x1, lnf_ref[...]).astype(BF16)
    ffn = wfg_ref.shape[1]
    chunk = 256
    for c in range(0, ffn, chunk):
        a = jnp.dot(h2, wfg_ref[:, c:c + chunk], preferred_element_type=F32)
        u = jnp.dot(h2, wfu_ref[:, c:c + chunk], preferred_element_type=F32)
        hid_ref[:, c:c + chunk] = (a * jax.nn.sigmoid(a) * u).astype(BF16)
    x2 = x1 + jnp.dot(hid_ref[...], wfd_ref[...], preferred_element_type=F32)

    h3 = _rms(x2, lnp_ref[...]).astype(BF16)
    g_ple = jax.nn.sigmoid(jnp.dot(h3, wpg_ref[...], preferred_element_type=F32))
    ple = jnp.dot(p_ref[...].astype(BF16), wpp_ref[...], preferred_element_type=F32)
    x3 = x2 + g_ple * ple
    o_ref[...] = _rms(x3, fin_ref[...])


def _out_ffn(x, o_sb, o_mb, gates, p, w_bs, w_bm, w_out, ln_ffn, w_fg, w_fu, w_fd, ln_ple, w_pg, w_pp,
             fin_g, tm):
    n, d = x.shape
    row = lambda width: pl.BlockSpec((tm, width), lambda i: (i, 0))

    def const(arr):
        return pl.BlockSpec(arr.shape, lambda i: (0, 0), pipeline_mode=pl.Buffered(1))

    weights = (w_bs, w_bm, w_out, ln_ffn, w_fg, w_fu, w_fd, ln_ple, w_pg, w_pp, fin_g)
    return pl.pallas_call(
        _out_ffn_kernel,
        grid=(n // tm,),
        in_specs=[row(d), row(o_sb.shape[1]), row(o_mb.shape[1]), row(gates.shape[1]), row(p.shape[1])]
        + [const(w) for w in weights],
        out_specs=row(d),
        out_shape=jax.ShapeDtypeStruct((n, d), F32),
        scratch_shapes=[pltpu.VMEM((tm, w_fg.shape[1]), BF16)],
        compiler_params=pltpu.CompilerParams(
            dimension_semantics=("arbitrary",), vmem_limit_bytes=VMEM_LIMIT_BYTES),
        name="out_ffn",
    )(x, o_sb, o_mb, gates, p, *weights)


def _t5_bucket_table(n):
    dist = np.arange(n)
    nf = np.maximum(dist, 1).astype(np.float32)
    large = MAX_EXACT + (np.log(nf / np.float32(MAX_EXACT)) / np.float32(math.log(MAX_DISTANCE / MAX_EXACT))
                         * np.float32(NUM_BUCKETS - MAX_EXACT)).astype(np.int32)
    large = np.minimum(large, NUM_BUCKETS - 1)
    return np.where(dist < MAX_EXACT, dist, large).astype(np.int32)


def _bucket_tiles():
    table = _t5_bucket_table(2 * BLK)
    s_idx = np.arange(BLK)[:, None]
    t_idx = np.arange(BLK)[None, :]
    own = table[np.maximum(t_idx - s_idx, 0)]
    prev = table[t_idx - s_idx + BLK]
    return np.stack([own, prev]).astype(np.int32)


def _suffix_matrix():
    t = (np.arange(HALF)[None, :] >= np.arange(HALF)[:, None]).astype(np.float32)
    return np.concatenate([t, t], axis=1)


def kernel(x, p, ln_mix_g, w_in, w_gate, b_gate, w_branch_sb, w_branch_moba, w_out, rel_bias, ln_ffn_g,
           w_ffn_gate, w_ffn_up, w_ffn_down, ln_ple_g, w_ple_gate, w_ple_proj, final_g):
    depth = w_in.shape[0]
    assert depth == 1, "the final RMSNorm is fused into the last layer's kernel"
    b, s, d = x.shape
    assert s % BLK == 0
    scale = HEAD_DIM ** -0.5
    sb_scale = scale * LOG2_E
    tri = jnp.asarray(_suffix_matrix(), BF16)
    buckets = jnp.asarray(_bucket_tiles())
    tm_in = min(512, s)
    tm_out = min(512, b * s)
    for i in range(depth):
        w = w_in[i]
        q_sb, k_sb, v_sb, q_mb, k_mb, v_mb = (w[:, j * MIX_WIDTH:(j + 1) * MIX_WIDTH] for j in range(6))
        w_qk = jnp.concatenate([q_sb * sb_scale, k_sb, q_mb * sb_scale, k_mb], axis=1).astype(BF16)
        w_vt = jnp.concatenate([v_sb, v_mb], axis=1).T.astype(BF16)
        qk, vt, gates = _in_proj(x, ln_mix_g[i][None], w_qk, w_vt, w_gate[i].astype(BF16), b_gate[i][None], tm_in)
        o_sb = _sb_attention(qk, vt, tri)
        o_mb = _moba_attention(qk, vt, rel_bias, buckets)
        x = _out_ffn(
            x.reshape(b * s, d), o_sb.reshape(b * s, MIX_WIDTH), o_mb.reshape(b * s, MIX_WIDTH),
            gates.reshape(b * s, 2 * d), p[i].reshape(b * s, -1),
            w_branch_sb[i].astype(BF16), w_branch_moba[i].astype(BF16), w_out[i].astype(BF16),
            ln_ffn_g[i][None], w_ffn_gate[i].astype(BF16), w_ffn_up[i].astype(BF16),
            w_ffn_down[i].astype(BF16), ln_ple_g[i][None], w_ple_gate[i].astype(BF16),
            w_ple_proj[i].astype(BF16), final_g[None], tm_out).reshape(b, s, d)
    return x
```

```python
import functools
import math

import jax
import jax.numpy as jnp
import numpy as np
from jax import lax
from jax.experimental import pallas as pl
from jax.experimental.pallas import tpu as pltpu

HEAD_DIM = 64
N_HEADS = 8
MIX_WIDTH = N_HEADS * HEAD_DIM
HEADS_PER_GROUP = 4
GROUP_WIDTH = HEADS_PER_GROUP * HEAD_DIM
N_GROUPS = N_HEADS // HEADS_PER_GROUP
BLK = 256
MOBA_TOPK = 3
NUM_BUCKETS = 32
MAX_EXACT = NUM_BUCKETS // 2
MAX_DISTANCE = 128
RMS_EPS = 1e-6
HALF = BLK // 2
LOG2_E = 1.0 / math.log(2.0)
OUT_SUBTILE = 512
NEAR_BLOCKS = 2
SB_SKIP_BITS = 160.0
NEG = -1e30
VMEM_LIMIT_BYTES = 56 * 1024 * 1024

F32 = jnp.float32
BF16 = jnp.bfloat16
NT_DIMS = (((1,), (1,)), ((), ()))


def _rms(x, g):
    return x * lax.rsqrt(jnp.mean(x * x, axis=-1, keepdims=True) + RMS_EPS) * g


def _in_proj_kernel(x_ref, g_ref, wqk_ref, wvt_ref, wg_ref, bg_ref, qk_ref, vt_ref, gate_ref):
    h = _rms(x_ref[...], g_ref[...]).astype(BF16)
    qk_ref[...] = jnp.dot(h, wqk_ref[...], preferred_element_type=F32).astype(BF16)
    vt_ref[...] = lax.dot_general(wvt_ref[...], h, NT_DIMS, preferred_element_type=F32).astype(BF16)
    gl = jnp.dot(h, wg_ref[...], preferred_element_type=F32) + bg_ref[...]
    gate_ref[...] = jax.nn.sigmoid(gl).astype(BF16)


def _in_proj(x, ln_g, w_qk, w_vt, w_gate, b_gate, tm):
    b, s, d = x.shape
    n_qk, n_v, n_g = w_qk.shape[1], w_vt.shape[0], w_gate.shape[1]

    def const(shape):
        return pl.BlockSpec(shape, lambda bi, ti: (0, 0), pipeline_mode=pl.Buffered(1))

    return pl.pallas_call(
        _in_proj_kernel,
        grid=(b, s // tm),
        in_specs=[
            pl.BlockSpec((None, tm, d), lambda bi, ti: (bi, ti, 0)),
            const((1, d)),
            const((d, n_qk)),
            const((n_v, d)),
            const((d, n_g)),
            const((1, n_g)),
        ],
        out_specs=[
            pl.BlockSpec((None, tm, n_qk), lambda bi, ti: (bi, ti, 0)),
            pl.BlockSpec((None, n_v, tm), lambda bi, ti: (bi, 0, ti)),
            pl.BlockSpec((None, tm, n_g), lambda bi, ti: (bi, ti, 0)),
        ],
        out_shape=[
            jax.ShapeDtypeStruct((b, s, n_qk), BF16),
            jax.ShapeDtypeStruct((b, n_v, s), BF16),
            jax.ShapeDtypeStruct((b, s, n_g), BF16),
        ],
        compiler_params=pltpu.CompilerParams(
            dimension_semantics=("arbitrary", "arbitrary"), vmem_limit_bytes=VMEM_LIMIT_BYTES),
        name="in_proj",
    )(x, ln_g, w_qk, w_vt, w_gate, b_gate)


def _sb_kernel(q_ref, k_ref, vt_ref, tri_ref, o_ref, acc_ref, carry_ref, qm_ref, more_ref):
    n_blk = q_ref.shape[0] // BLK
    tri = tri_ref[...]
    lane_head = lax.broadcasted_iota(jnp.int32, (BLK, GROUP_WIDTH), 1) // HEAD_DIM
    heads = range(HEADS_PER_GROUP)
    aligned = lambda x: x if isinstance(x, int) else pl.multiple_of(x, BLK)

    def masked_queries(i):
        qb = q_ref[pl.ds(aligned(i * BLK), BLK), :]
        return [jnp.where(lane_head == h, qb, jnp.zeros_like(qb)) for h in heads]

    def tile(qm, k0, n_kb, carry, diagonal):
        n_keys, n_half = n_kb * BLK, 2 * n_kb
        keys = k_ref[pl.ds(k0, n_keys), :]

        def mask_diagonal(x):
            s_idx = lax.broadcasted_iota(jnp.int32, (BLK, BLK), 0)
            t_idx = lax.broadcasted_iota(jnp.int32, (BLK, BLK), 1)
            last = jnp.where(s_idx < t_idx, x[n_keys - BLK:], 0.0)
            return last if n_kb == 1 else jnp.concatenate([x[:n_keys - BLK], last], axis=0)

        z = [lax.dot_general(keys, qm[h], NT_DIMS, preferred_element_type=F32) for h in heads]
        hi, lo = [], []
        for h in heads:
            a = jnp.maximum(z[h], 0.0) + jnp.log(1.0 + jnp.exp2(-jnp.abs(z[h]))) * LOG2_E
            if diagonal:
                a = mask_diagonal(a)
            hi.append(a.astype(BF16))
            lo.append((a - hi[h].astype(F32)).astype(BF16))
        half = lambda x, c: x[c * HALF:(c + 1) * HALF]
        inc = [[jnp.dot(tri, jnp.concatenate([half(hi[h], c), half(lo[h], c)], axis=0),
                        preferred_element_type=F32) for c in range(n_half)] for h in heads]
        pvs, new_carry = [], []
        for h in heads:
            row, args = carry[h], [None] * n_half
            for c in reversed(range(n_half)):
                args[c] = half(z[h], c) - inc[h][c] - row
                row = row + inc[h][c][0:1, :]
            w = jnp.exp2(jnp.concatenate(args, axis=0))
            if diagonal:
                w = mask_diagonal(w)
            rows = slice(h * HEAD_DIM, (h + 1) * HEAD_DIM)
            pvs.append(jnp.dot(vt_ref[rows, pl.ds(k0, n_keys)], w.astype(BF16), preferred_element_type=F32))
            new_carry.append(row)
        return jnp.concatenate(pvs, axis=0), tuple(new_carry)

    def live(carry):
        return jnp.min(jnp.concatenate(carry, axis=0)) < SB_SKIP_BITS

    for i in range(n_blk):
        n_near = min(i + 1, NEAR_BLOCKS)
        pv, carry = tile(masked_queries(i), (i + 1 - n_near) * BLK, n_near,
                         tuple(jnp.zeros((1, BLK), F32) for _ in heads), True)
        o_ref[i * BLK:(i + 1) * BLK, :] = pv.T.astype(BF16)
        unvisited = i + 1 - n_near
        if unvisited > 0:
            acc_ref[i] = pv
            carry_ref[i] = jnp.concatenate(carry, axis=0)
            more_ref[i] = live(carry).astype(jnp.int32)

    def far_blocks(i, _):
        @pl.when(more_ref[i] != 0)
        def _():
            first = i + 1 - NEAR_BLOCKS
            qm = masked_queries(i)
            for h in heads:
                qm_ref[h] = qm[h]
            c0 = carry_ref[i]

            def cond(state):
                jj, go, _ = state
                return (jj < first) & go

            def body(state):
                jj, _, c = state
                pv, c = tile([qm_ref[h] for h in heads], aligned((first - 1 - jj) * BLK), 1, c, False)
                acc_ref[i] += pv
                return jj + 1, live(c), c

            lax.while_loop(cond, body, (0, True, tuple(c0[h:h + 1, :] for h in heads)))
            o_ref[pl.ds(aligned(i * BLK), BLK), :] = acc_ref[i].T.astype(BF16)

        return 0

    lax.fori_loop(NEAR_BLOCKS, n_blk, far_blocks, 0)


def _sb_attention(qk, vt, tri):
    b, s, _ = qk.shape
    return pl.pallas_call(
        _sb_kernel,
        grid=(b, N_GROUPS),
        in_specs=[
            pl.BlockSpec((None, s, GROUP_WIDTH), lambda bi, gi: (bi, 0, gi)),
            pl.BlockSpec((None, s, GROUP_WIDTH), lambda bi, gi: (bi, 0, N_GROUPS + gi)),
            pl.BlockSpec((None, GROUP_WIDTH, s), lambda bi, gi: (bi, gi, 0)),
            pl.BlockSpec((HALF, 2 * HALF), lambda bi, gi: (0, 0)),
        ],
        out_specs=pl.BlockSpec((None, s, GROUP_WIDTH), lambda bi, gi: (bi, 0, gi)),
        out_shape=jax.ShapeDtypeStruct((b, s, MIX_WIDTH), BF16),
        scratch_shapes=[
            pltpu.VMEM((s // BLK, GROUP_WIDTH, BLK), F32),
            pltpu.VMEM((s // BLK, HEADS_PER_GROUP, BLK), F32),
            pltpu.VMEM((HEADS_PER_GROUP, BLK, GROUP_WIDTH), BF16),
            pltpu.SMEM((s // BLK,), jnp.int32),
        ],
        compiler_params=pltpu.CompilerParams(
            dimension_semantics=("arbitrary", "arbitrary"), vmem_limit_bytes=VMEM_LIMIT_BYTES),
        name="sb_attn",
    )(qk, qk, vt, tri)


def _moba_kernel(bias_ref, q_ref, k_ref, vt_ref, bkt_ref, o_ref, btile_ref, madd_ref):
    gi = pl.program_id(0)
    bi = pl.program_id(1)
    s_len = q_ref.shape[0]
    n_blk = s_len // BLK
    far_bucket = NUM_BUCKETS - 1

    @pl.when(bi == 0)
    def _():
        for which in range(2):
            bkt = bkt_ref[which]
            for h in range(HEADS_PER_GROUP):
                t = jnp.zeros((BLK, BLK), F32)
                for bu in range(NUM_BUCKETS):
                    t = jnp.where(bkt == bu, bias_ref[bu, gi * HEADS_PER_GROUP + h] * LOG2_E, t)
                btile_ref[which, h] = t

    kbar = jnp.concatenate(
        [jnp.sum(k_ref[n * BLK:(n + 1) * BLK, :].astype(F32), axis=0, keepdims=True) for n in range(n_blk)],
        axis=0) * (1.0 / BLK)
    lane_head = lax.broadcasted_iota(jnp.int32, kbar.shape, 1) // HEAD_DIM
    pieces, rem = [], kbar
    for _ in range(3):
        p = rem.astype(BF16).astype(F32)
        pieces.append(p)
        rem = rem - p
    rows = [jnp.where(lane_head == h, p, 0.0) for p in pieces for h in range(HEADS_PER_GROUP)]
    r = jnp.concatenate(rows, axis=0).astype(BF16)
    gt = lax.dot_general(r, q_ref[...], NT_DIMS, preferred_element_type=F32)
    npc = HEADS_PER_GROUP * n_blk
    gate = gt[0:npc] + gt[npc:2 * npc] + gt[2 * npc:3 * npc]
    blk_idx = lax.broadcasted_iota(jnp.int32, (n_blk, s_len), 0)
    cur_blk = lax.broadcasted_iota(jnp.int32, (n_blk, s_len), 1) // BLK
    past = blk_idx < cur_blk
    for h in range(HEADS_PER_GROUP):
        g = jnp.where(past, gate[h * n_blk:(h + 1) * n_blk], -jnp.inf)
        beaten_by = jnp.zeros((n_blk, s_len), jnp.int32)
        for m in range(n_blk):
            gm = g[m:m + 1, :]
            beats = (gm > g) | ((gm == g) & (m < blk_idx))
            beaten_by = beaten_by + beats.astype(jnp.int32)
        selected = (beaten_by < MOBA_TOPK) & past
        madd_ref[h] = jnp.where(selected, 0.0, NEG)

    s_idx = lax.broadcasted_iota(jnp.int32, (BLK, BLK), 0)
    t_idx = lax.broadcasted_iota(jnp.int32, (BLK, BLK), 1)
    causal = s_idx <= t_idx
    lane_head_q = lax.broadcasted_iota(jnp.int32, (BLK, GROUP_WIDTH), 1) // HEAD_DIM

    heads = range(HEADS_PER_GROUP)
    far_bias = [bias_ref[far_bucket, gi * HEADS_PER_GROUP + h] * LOG2_E for h in heads]

    for i in range(n_blk):
        q_lo, n_keys = i * BLK, (i + 1) * BLK
        qb = q_ref[q_lo:q_lo + BLK, :]
        keys = k_ref[0:n_keys, :]
        z = [lax.dot_general(keys, jnp.where(lane_head_q == h, qb, jnp.zeros_like(qb)), NT_DIMS,
                             preferred_element_type=F32) for h in heads]
        out_rows = []
        for h in heads:
            madd = madd_ref[h, :, q_lo:q_lo + BLK]
            blocks, adds = [], []
            for j in range(i + 1):
                zj = z[h][j * BLK:(j + 1) * BLK]
                if j == i:
                    blocks.append(jnp.where(causal, zj + btile_ref[0, h], NEG))
                    adds.append(None)
                elif j == i - 1:
                    blocks.append(zj + btile_ref[1, h])
                    adds.append(madd[j:j + 1, :])
                else:
                    blocks.append(zj)
                    adds.append(madd[j:j + 1, :] + far_bias[h])
            mx = None
            for blk, add in zip(blocks, adds):
                cm = jnp.max(blk, axis=0, keepdims=True)
                cm = cm if add is None else cm + add
                mx = cm if mx is None else jnp.maximum(mx, cm)
            lsum, probs = None, []
            for blk, add in zip(blocks, adds):
                p = jnp.exp2(blk + ((-mx) if add is None else (add - mx)))
                ps = jnp.sum(p, axis=0, keepdims=True)
                lsum = ps if lsum is None else lsum + ps
                probs.append(p.astype(BF16))
            rows = slice(h * HEAD_DIM, (h + 1) * HEAD_DIM)
            pv = jnp.dot(vt_ref[rows, 0:n_keys], jnp.concatenate(probs, axis=0), preferred_element_type=F32)
            out_rows.append(pv * (1.0 / lsum))
        o_ref[q_lo:q_lo + BLK, :] = jnp.concatenate(out_rows, axis=0).T.astype(BF16)


def _moba_attention(qk, vt, rel_bias, buckets):
    b, s, _ = qk.shape
    n_blk = s // BLK
    return pl.pallas_call(
        _moba_kernel,
        grid=(N_GROUPS, b),
        in_specs=[
            pl.BlockSpec(memory_space=pltpu.SMEM),
            pl.BlockSpec((None, s, GROUP_WIDTH), lambda gi, bi: (bi, 0, 2 * N_GROUPS + gi)),
            pl.BlockSpec((None, s, GROUP_WIDTH), lambda gi, bi: (bi, 0, 3 * N_GROUPS + gi)),
            pl.BlockSpec((None, GROUP_WIDTH, s), lambda gi, bi: (bi, N_GROUPS + gi, 0)),
            pl.BlockSpec((2, BLK, BLK), lambda gi, bi: (0, 0, 0)),
        ],
        out_specs=pl.BlockSpec((None, s, GROUP_WIDTH), lambda gi, bi: (bi, 0, gi)),
        out_shape=jax.ShapeDtypeStruct((b, s, MIX_WIDTH), BF16),
        scratch_shapes=[
            pltpu.VMEM((2, HEADS_PER_GROUP, BLK, BLK), F32),
            pltpu.VMEM((HEADS_PER_GROUP, n_blk, s), F32),
        ],
        compiler_params=pltpu.CompilerParams(
            dimension_semantics=("arbitrary", "arbitrary"), vmem_limit_bytes=VMEM_LIMIT_BYTES),
        name="moba_attn",
    )(rel_bias, qk, qk, vt, buckets)


def _out_ffn_kernel(x_ref, osb_ref, omb_ref, gate_ref, p_ref, wbs_ref, wbm_ref, wo_ref, lnf_ref,
                    wfg_ref, wfu_ref, wfd_ref, lnp_ref, wpg_ref, wpp_ref, fin_ref, o_ref, hid_ref):
    tm, d = x_ref.shape
    ffn = wfg_ref.shape[1]
    chunk = 256
    for r in range(0, tm, OUT_SUBTILE):
        rows = slice(r, r + OUT_SUBTILE)
        y_sb = jnp.dot(osb_ref[rows, :], wbs_ref[...], preferred_element_type=F32)
        y_mb = jnp.dot(omb_ref[rows, :], wbm_ref[...], preferred_element_type=F32)
        mix = gate_ref[rows, :d].astype(F32) * y_sb + gate_ref[rows, d:].astype(F32) * y_mb
        x1 = x_ref[rows, :] + jnp.dot(mix.astype(BF16), wo_ref[...], preferred_element_type=F32)

        h2 = _rms(x1, lnf_ref[...]).astype(BF16)
        for c in range(0, ffn, chunk):
            a = jnp.dot(h2, wfg_ref[:, c:c + chunk], preferred_element_type=F32)
            u = jnp.dot(h2, wfu_ref[:, c:c + chunk], preferred_element_type=F32)
            hid_ref[rows, c:c + chunk] = (a * jax.nn.sigmoid(a) * u).astype(BF16)
        x2 = x1 + jnp.dot(hid_ref[rows, :], wfd_ref[...], preferred_element_type=F32)

        h3 = _rms(x2, lnp_ref[...]).astype(BF16)
        g_ple = jax.nn.sigmoid(jnp.dot(h3, wpg_ref[...], preferred_element_type=F32))
        ple = jnp.dot(p_ref[rows, :].astype(BF16), wpp_ref[...], preferred_element_type=F32)
        x3 = x2 + g_ple * ple
        o_ref[rows, :] = _rms(x3, fin_ref[...])


def _out_ffn(x, o_sb, o_mb, gates, p, w_bs, w_bm, w_out, ln_ffn, w_fg, w_fu, w_fd, ln_ple, w_pg, w_pp,
             fin_g, tm):
    n, d = x.shape
    row = lambda width: pl.BlockSpec((tm, width), lambda i: (i, 0))

    def const(arr):
        return pl.BlockSpec(arr.shape, lambda i: (0, 0), pipeline_mode=pl.Buffered(1))

    weights = (w_bs, w_bm, w_out, ln_ffn, w_fg, w_fu, w_fd, ln_ple, w_pg, w_pp, fin_g)
    return pl.pallas_call(
        _out_ffn_kernel,
        grid=(n // tm,),
        in_specs=[row(d), row(o_sb.shape[1]), row(o_mb.shape[1]), row(gates.shape[1]), row(p.shape[1])]
        + [const(w) for w in weights],
        out_specs=row(d),
        out_shape=jax.ShapeDtypeStruct((n, d), F32),
        scratch_shapes=[pltpu.VMEM((tm, w_fg.shape[1]), BF16)],
        compiler_params=pltpu.CompilerParams(
            dimension_semantics=("arbitrary",), vmem_limit_bytes=VMEM_LIMIT_BYTES),
        name="out_ffn",
    )(x, o_sb, o_mb, gates, p, *weights)


def _t5_bucket_table(n):
    dist = np.arange(n)
    nf = np.maximum(dist, 1).astype(np.float32)
    large = MAX_EXACT + (np.log(nf / np.float32(MAX_EXACT)) / np.float32(math.log(MAX_DISTANCE / MAX_EXACT))
                         * np.float32(NUM_BUCKETS - MAX_EXACT)).astype(np.int32)
    large = np.minimum(large, NUM_BUCKETS - 1)
    return np.where(dist < MAX_EXACT, dist, large).astype(np.int32)


def _bucket_tiles():
    table = _t5_bucket_table(2 * BLK)
    s_idx = np.arange(BLK)[:, None]
    t_idx = np.arange(BLK)[None, :]
    own = table[np.maximum(t_idx - s_idx, 0)]
    prev = table[t_idx - s_idx + BLK]
    return np.stack([own, prev]).astype(np.int32)


def _suffix_matrix():
    t = (np.arange(HALF)[None, :] >= np.arange(HALF)[:, None]).astype(np.float32)
    return np.concatenate([t, t], axis=1)


def kernel(x, p, ln_mix_g, w_in, w_gate, b_gate, w_branch_sb, w_branch_moba, w_out, rel_bias, ln_ffn_g,
           w_ffn_gate, w_ffn_up, w_ffn_down, ln_ple_g, w_ple_gate, w_ple_proj, final_g):
    depth = w_in.shape[0]
    assert depth == 1, "the final RMSNorm is fused into the last layer's kernel"
    b, s, d = x.shape
    assert s % BLK == 0
    scale = HEAD_DIM ** -0.5
    sb_scale = scale * LOG2_E
    tri = jnp.asarray(_suffix_matrix(), BF16)
    buckets = jnp.asarray(_bucket_tiles())
    tm_in = min(512, s)
    tm_out = min(512, b * s)
    for i in range(depth):
        w = w_in[i]
        q_sb, k_sb, v_sb, q_mb, k_mb, v_mb = (w[:, j * MIX_WIDTH:(j + 1) * MIX_WIDTH] for j in range(6))
        w_qk = jnp.concatenate([q_sb * sb_scale, k_sb, q_mb * sb_scale, k_mb], axis=1).astype(BF16)
        w_vt = jnp.concatenate([v_sb, v_mb], axis=1).T.astype(BF16)
        qk, vt, gates = _in_proj(x, ln_mix_g[i][None], w_qk, w_vt, w_gate[i].astype(BF16), b_gate[i][None], tm_in)
        o_sb = _sb_attention(qk, vt, tri)
        o_mb = _moba_attention(qk, vt, rel_bias, buckets)
        x = _out_ffn(
            x.reshape(b * s, d), o_sb.reshape(b * s, MIX_WIDTH), o_mb.reshape(b * s, MIX_WIDTH),
            gates.reshape(b * s, 2 * d), p[i].reshape(b * s, -1),
            w_branch_sb[i].astype(BF16), w_branch_moba[i].astype(BF16), w_out[i].astype(BF16),
            ln_ffn_g[i][None], w_ffn_gate[i].astype(BF16), w_ffn_up[i].astype(BF16),
            w_ffn_down[i].astype(BF16), ln_ple_g[i][None], w_ple_gate[i].astype(BF16),
            w_ple_proj[i].astype(BF16), final_g[None], tm_out).reshape(b, s, d)
    return x
```

```python
import functools
import math

import jax
import jax.numpy as jnp
import numpy as np
from jax import lax
from jax.experimental import pallas as pl
from jax.experimental.pallas import tpu as pltpu

HEAD_DIM = 64
N_HEADS = 8
MIX_WIDTH = N_HEADS * HEAD_DIM
HEADS_PER_GROUP = 4
GROUP_WIDTH = HEADS_PER_GROUP * HEAD_DIM
N_GROUPS = N_HEADS // HEADS_PER_GROUP
BLK = 256
MOBA_TOPK = 3
NUM_BUCKETS = 32
MAX_EXACT = NUM_BUCKETS // 2
MAX_DISTANCE = 128
RMS_EPS = 1e-6
HALF = BLK // 2
LOG2_E = 1.0 / math.log(2.0)
MOBA_SKEW = 1
OUT_SUBTILE = 512
NEAR_BLOCKS = 2
SB_SKIP_BITS = 160.0
NEG = -1e30
VMEM_LIMIT_BYTES = 56 * 1024 * 1024

F32 = jnp.float32
BF16 = jnp.bfloat16
NT_DIMS = (((1,), (1,)), ((), ()))


def _rms(x, g):
    return x * lax.rsqrt(jnp.mean(x * x, axis=-1, keepdims=True) + RMS_EPS) * g


def _in_proj_kernel(x_ref, g_ref, wqk_ref, wvt_ref, wg_ref, bg_ref, qk_ref, vt_ref, gate_ref):
    h = _rms(x_ref[...], g_ref[...]).astype(BF16)
    qk_ref[...] = jnp.dot(h, wqk_ref[...], preferred_element_type=F32).astype(BF16)
    vt_ref[...] = lax.dot_general(wvt_ref[...], h, NT_DIMS, preferred_element_type=F32).astype(BF16)
    gl = jnp.dot(h, wg_ref[...], preferred_element_type=F32) + bg_ref[...]
    gate_ref[...] = jax.nn.sigmoid(gl).astype(BF16)


def _in_proj(x, ln_g, w_qk, w_vt, w_gate, b_gate, tm):
    b, s, d = x.shape
    n_qk, n_v, n_g = w_qk.shape[1], w_vt.shape[0], w_gate.shape[1]

    def const(shape):
        return pl.BlockSpec(shape, lambda bi, ti: (0, 0), pipeline_mode=pl.Buffered(1))

    return pl.pallas_call(
        _in_proj_kernel,
        grid=(b, s // tm),
        in_specs=[
            pl.BlockSpec((None, tm, d), lambda bi, ti: (bi, ti, 0)),
            const((1, d)),
            const((d, n_qk)),
            const((n_v, d)),
            const((d, n_g)),
            const((1, n_g)),
        ],
        out_specs=[
            pl.BlockSpec((None, tm, n_qk), lambda bi, ti: (bi, ti, 0)),
            pl.BlockSpec((None, n_v, tm), lambda bi, ti: (bi, 0, ti)),
            pl.BlockSpec((None, tm, n_g), lambda bi, ti: (bi, ti, 0)),
        ],
        out_shape=[
            jax.ShapeDtypeStruct((b, s, n_qk), BF16),
            jax.ShapeDtypeStruct((b, n_v, s), BF16),
            jax.ShapeDtypeStruct((b, s, n_g), BF16),
        ],
        compiler_params=pltpu.CompilerParams(
            dimension_semantics=("arbitrary", "arbitrary"), vmem_limit_bytes=VMEM_LIMIT_BYTES),
        name="in_proj",
    )(x, ln_g, w_qk, w_vt, w_gate, b_gate)


def _sb_kernel(q_ref, k_ref, vt_ref, tri_ref, o_ref, acc_ref, carry_ref, qm_ref, more_ref):
    n_blk = q_ref.shape[0] // BLK
    tri = tri_ref[...]
    lane_head = lax.broadcasted_iota(jnp.int32, (BLK, GROUP_WIDTH), 1) // HEAD_DIM
    heads = range(HEADS_PER_GROUP)
    aligned = lambda x: x if isinstance(x, int) else pl.multiple_of(x, BLK)

    def masked_queries(i):
        qb = q_ref[pl.ds(aligned(i * BLK), BLK), :]
        return [jnp.where(lane_head == h, qb, jnp.zeros_like(qb)) for h in heads]

    half = lambda x, c: x[c * HALF:(c + 1) * HALF]

    class Job:
        def __init__(self, h, q_masked, k0, n_kb, carry, diagonal):
            self.h, self.q, self.k0, self.n_kb, self.carry, self.diagonal = h, q_masked, k0, n_kb, carry, diagonal

        def mask(self, x):
            n_keys = self.n_kb * BLK
            s_idx = lax.broadcasted_iota(jnp.int32, (BLK, BLK), 0)
            t_idx = lax.broadcasted_iota(jnp.int32, (BLK, BLK), 1)
            last = jnp.where(s_idx < t_idx, x[n_keys - BLK:], 0.0)
            return last if self.n_kb == 1 else jnp.concatenate([x[:n_keys - BLK], last], axis=0)

    def stage_scores(job):
        keys = k_ref[pl.ds(job.k0, job.n_kb * BLK), :]
        job.z = lax.dot_general(keys, job.q, NT_DIMS, preferred_element_type=F32)

    def stage_split(job):
        a = jnp.maximum(job.z, 0.0) + jnp.log(1.0 + jnp.exp2(-jnp.abs(job.z))) * LOG2_E
        if job.diagonal:
            a = job.mask(a)
        job.hi = a.astype(BF16)
        job.lo = (a - job.hi.astype(F32)).astype(BF16)

    def stage_suffix(job):
        job.inc = [jnp.dot(tri, jnp.concatenate([half(job.hi, c), half(job.lo, c)], axis=0),
                           preferred_element_type=F32) for c in range(2 * job.n_kb)]

    def stage_weights(job):
        n_half = 2 * job.n_kb
        row, args = job.carry, [None] * n_half
        for c in reversed(range(n_half)):
            args[c] = half(job.z, c) - job.inc[c] - row
            row = row + job.inc[c][0:1, :]
        w = jnp.exp2(jnp.concatenate(args, axis=0))
        job.w = (job.mask(w) if job.diagonal else w).astype(BF16)
        job.carry = row

    def stage_pv(job):
        rows = slice(job.h * HEAD_DIM, (job.h + 1) * HEAD_DIM)
        job.pv = jnp.dot(vt_ref[rows, pl.ds(job.k0, job.n_kb * BLK)], job.w, preferred_element_type=F32)

    stages = (stage_scores, stage_split, stage_suffix, stage_weights, stage_pv)

    def live(carry):
        return jnp.min(jnp.concatenate(carry, axis=0)) < SB_SKIP_BITS

    jobs = []
    for i in range(n_blk):
        n_near = min(i + 1, NEAR_BLOCKS)
        qm = masked_queries(i)
        jobs += [Job(h, qm[h], (i + 1 - n_near) * BLK, n_near, jnp.zeros((1, BLK), F32), True) for h in heads]
    for step in range(len(jobs) + len(stages) - 1):
        for s, stage in enumerate(stages):
            if 0 <= step - s < len(jobs):
                stage(jobs[step - s])
        done = step - (len(stages) - 1)
        if done >= 0 and done % HEADS_PER_GROUP == HEADS_PER_GROUP - 1:
            i = done // HEADS_PER_GROUP
            block = jobs[done - HEADS_PER_GROUP + 1:done + 1]
            pv = jnp.concatenate([job.pv for job in block], axis=0)
            o_ref[i * BLK:(i + 1) * BLK, :] = pv.T.astype(BF16)
            if i + 1 > NEAR_BLOCKS:
                carry = [job.carry for job in block]
                acc_ref[i] = pv
                carry_ref[i] = jnp.concatenate(carry, axis=0)
                more_ref[i] = live(carry).astype(jnp.int32)

    def far_blocks(i, _):
        @pl.when(more_ref[i] != 0)
        def _():
            first = i + 1 - NEAR_BLOCKS
            qm = masked_queries(i)
            for h in heads:
                qm_ref[h] = qm[h]
            c0 = carry_ref[i]

            def cond(state):
                jj, go, _ = state
                return (jj < first) & go

            def body(state):
                jj, _, c = state
                k0 = aligned((first - 1 - jj) * BLK)
                block = [Job(h, qm_ref[h], k0, 1, c[h], False) for h in heads]
                for stage in stages:
                    for job in block:
                        stage(job)
                acc_ref[i] += jnp.concatenate([job.pv for job in block], axis=0)
                c = tuple(job.carry for job in block)
                return jj + 1, live(c), c

            lax.while_loop(cond, body, (0, True, tuple(c0[h:h + 1, :] for h in heads)))
            o_ref[pl.ds(aligned(i * BLK), BLK), :] = acc_ref[i].T.astype(BF16)

        return 0

    lax.fori_loop(NEAR_BLOCKS, n_blk, far_blocks, 0)


def _sb_attention(qk, vt, tri):
    b, s, _ = qk.shape
    return pl.pallas_call(
        _sb_kernel,
        grid=(b, N_GROUPS),
        in_specs=[
            pl.BlockSpec((None, s, GROUP_WIDTH), lambda bi, gi: (bi, 0, gi)),
            pl.BlockSpec((None, s, GROUP_WIDTH), lambda bi, gi: (bi, 0, N_GROUPS + gi)),
            pl.BlockSpec((None, GROUP_WIDTH, s), lambda bi, gi: (bi, gi, 0)),
            pl.BlockSpec((HALF, 2 * HALF), lambda bi, gi: (0, 0)),
        ],
        out_specs=pl.BlockSpec((None, s, GROUP_WIDTH), lambda bi, gi: (bi, 0, gi)),
        out_shape=jax.ShapeDtypeStruct((b, s, MIX_WIDTH), BF16),
        scratch_shapes=[
            pltpu.VMEM((s // BLK, GROUP_WIDTH, BLK), F32),
            pltpu.VMEM((s // BLK, HEADS_PER_GROUP, BLK), F32),
            pltpu.VMEM((HEADS_PER_GROUP, BLK, GROUP_WIDTH), BF16),
            pltpu.SMEM((s // BLK,), jnp.int32),
        ],
        compiler_params=pltpu.CompilerParams(
            dimension_semantics=("arbitrary", "arbitrary"), vmem_limit_bytes=VMEM_LIMIT_BYTES),
        name="sb_attn",
    )(qk, qk, vt, tri)


def _moba_kernel(bias_ref, q_ref, k_ref, vt_ref, bkt_ref, o_ref, btile_ref, madd_ref):
    gi = pl.program_id(0)
    bi = pl.program_id(1)
    s_len = q_ref.shape[0]
    n_blk = s_len // BLK
    far_bucket = NUM_BUCKETS - 1

    @pl.when(bi == 0)
    def _():
        for which in range(2):
            bkt = bkt_ref[which]
            for h in range(HEADS_PER_GROUP):
                t = jnp.zeros((BLK, BLK), F32)
                for bu in range(NUM_BUCKETS):
                    t = jnp.where(bkt == bu, bias_ref[bu, gi * HEADS_PER_GROUP + h] * LOG2_E, t)
                btile_ref[which, h] = t

    kbar = jnp.concatenate(
        [jnp.sum(k_ref[n * BLK:(n + 1) * BLK, :].astype(F32), axis=0, keepdims=True) for n in range(n_blk)],
        axis=0) * (1.0 / BLK)
    lane_head = lax.broadcasted_iota(jnp.int32, kbar.shape, 1) // HEAD_DIM
    pieces, rem = [], kbar
    for _ in range(3):
        p = rem.astype(BF16).astype(F32)
        pieces.append(p)
        rem = rem - p
    rows = [jnp.where(lane_head == h, p, 0.0) for p in pieces for h in range(HEADS_PER_GROUP)]
    r = jnp.concatenate(rows, axis=0).astype(BF16)
    gt = lax.dot_general(r, q_ref[...], NT_DIMS, preferred_element_type=F32)
    npc = HEADS_PER_GROUP * n_blk
    gate = gt[0:npc] + gt[npc:2 * npc] + gt[2 * npc:3 * npc]
    blk_idx = lax.broadcasted_iota(jnp.int32, (n_blk, s_len), 0)
    cur_blk = lax.broadcasted_iota(jnp.int32, (n_blk, s_len), 1) // BLK
    past = blk_idx < cur_blk
    for h in range(HEADS_PER_GROUP):
        g = jnp.where(past, gate[h * n_blk:(h + 1) * n_blk], -jnp.inf)
        beaten_by = jnp.zeros((n_blk, s_len), jnp.int32)
        for m in range(n_blk):
            gm = g[m:m + 1, :]
            beats = (gm > g) | ((gm == g) & (m < blk_idx))
            beaten_by = beaten_by + beats.astype(jnp.int32)
        selected = (beaten_by < MOBA_TOPK) & past
        madd_ref[h] = jnp.where(selected, 0.0, NEG)

    s_idx = lax.broadcasted_iota(jnp.int32, (BLK, BLK), 0)
    t_idx = lax.broadcasted_iota(jnp.int32, (BLK, BLK), 1)
    causal = s_idx <= t_idx
    lane_head_q = lax.broadcasted_iota(jnp.int32, (BLK, GROUP_WIDTH), 1) // HEAD_DIM

    heads = range(HEADS_PER_GROUP)
    far_bias = [bias_ref[far_bucket, gi * HEADS_PER_GROUP + h] * LOG2_E for h in heads]

    state, out_rows = {}, {}

    def stage_scores(i, h):
        qb = q_ref[i * BLK:(i + 1) * BLK, :]
        qh = jnp.where(lane_head_q == h, qb, jnp.zeros_like(qb))
        state[i, h] = lax.dot_general(k_ref[0:(i + 1) * BLK, :], qh, NT_DIMS,
                                      preferred_element_type=F32)

    def stage_max(i, h):
        z = state[i, h]
        madd = madd_ref[h, :, i * BLK:(i + 1) * BLK]
        blocks, adds = [], []
        for j in range(i + 1):
            zj = z[j * BLK:(j + 1) * BLK]
            if j == i:
                blocks.append(jnp.where(causal, zj + btile_ref[0, h], NEG))
                adds.append(None)
            elif j == i - 1:
                blocks.append(zj + btile_ref[1, h])
                adds.append(madd[j:j + 1, :])
            else:
                blocks.append(zj)
                adds.append(madd[j:j + 1, :] + far_bias[h])
        mx = None
        for blk, add in zip(blocks, adds):
            cm = jnp.max(blk, axis=0, keepdims=True)
            cm = cm if add is None else cm + add
            mx = cm if mx is None else jnp.maximum(mx, cm)
        state[i, h] = (blocks, adds, mx)

    def stage_exp(i, h):
        blocks, adds, mx = state[i, h]
        lsum, probs = None, []
        for blk, add in zip(blocks, adds):
            p = jnp.exp2(blk + ((-mx) if add is None else (add - mx)))
            ps = jnp.sum(p, axis=0, keepdims=True)
            lsum = ps if lsum is None else lsum + ps
            probs.append(p.astype(BF16))
        state[i, h] = (jnp.concatenate(probs, axis=0), lsum)

    def stage_pv(i, h):
        probs, lsum = state.pop((i, h))
        rows = slice(h * HEAD_DIM, (h + 1) * HEAD_DIM)
        pv = jnp.dot(vt_ref[rows, 0:(i + 1) * BLK], probs, preferred_element_type=F32)
        out_rows.setdefault(i, []).append(pv * (1.0 / lsum))
        if h == HEADS_PER_GROUP - 1:
            o_ref[i * BLK:(i + 1) * BLK, :] = jnp.concatenate(out_rows.pop(i), axis=0).T.astype(BF16)

    items = [(i, h) for i in range(n_blk) for h in heads]
    stages = (stage_scores, stage_max, stage_exp, stage_pv)
    for step in range(len(items) + (len(stages) - 1) * MOBA_SKEW):
        for s, stage in enumerate(stages):
            if 0 <= step - s * MOBA_SKEW < len(items):
                stage(*items[step - s * MOBA_SKEW])


def _moba_attention(qk, vt, rel_bias, buckets):
    b, s, _ = qk.shape
    n_blk = s // BLK
    return pl.pallas_call(
        _moba_kernel,
        grid=(N_GROUPS, b),
        in_specs=[
            pl.BlockSpec(memory_space=pltpu.SMEM),
            pl.BlockSpec((None, s, GROUP_WIDTH), lambda gi, bi: (bi, 0, 2 * N_GROUPS + gi)),
            pl.BlockSpec((None, s, GROUP_WIDTH), lambda gi, bi: (bi, 0, 3 * N_GROUPS + gi)),
            pl.BlockSpec((None, GROUP_WIDTH, s), lambda gi, bi: (bi, N_GROUPS + gi, 0)),
            pl.BlockSpec((2, BLK, BLK), lambda gi, bi: (0, 0, 0)),
        ],
        out_specs=pl.BlockSpec((None, s, GROUP_WIDTH), lambda gi, bi: (bi, 0, gi)),
        out_shape=jax.ShapeDtypeStruct((b, s, MIX_WIDTH), BF16),
        scratch_shapes=[
            pltpu.VMEM((2, HEADS_PER_GROUP, BLK, BLK), F32),
            pltpu.VMEM((HEADS_PER_GROUP, n_blk, s), F32),
        ],
        compiler_params=pltpu.CompilerParams(
            dimension_semantics=("arbitrary", "arbitrary"), vmem_limit_bytes=VMEM_LIMIT_BYTES),
        name="moba_attn",
    )(rel_bias, qk, qk, vt, buckets)


def _out_ffn_kernel(x_ref, osb_ref, omb_ref, gate_ref, p_ref, wbs_ref, wbm_ref, wo_ref, lnf_ref,
                    wfg_ref, wfu_ref, wfd_ref, lnp_ref, wpg_ref, wpp_ref, fin_ref, o_ref, hid_ref):
    tm, d = x_ref.shape
    ffn = wfg_ref.shape[1]
    chunk = 256
    for r in range(0, tm, OUT_SUBTILE):
        rows = slice(r, r + OUT_SUBTILE)
        y_sb = jnp.dot(osb_ref[rows, :], wbs_ref[...], preferred_element_type=F32)
        y_mb = jnp.dot(omb_ref[rows, :], wbm_ref[...], preferred_element_type=F32)
        mix = gate_ref[rows, :d].astype(F32) * y_sb + gate_ref[rows, d:].astype(F32) * y_mb
        x1 = x_ref[rows, :] + jnp.dot(mix.astype(BF16), wo_ref[...], preferred_element_type=F32)

        h2 = _rms(x1, lnf_ref[...]).astype(BF16)
        for c in range(0, ffn, chunk):
            a = jnp.dot(h2, wfg_ref[:, c:c + chunk], preferred_element_type=F32)
            u = jnp.dot(h2, wfu_ref[:, c:c + chunk], preferred_element_type=F32)
            hid_ref[rows, c:c + chunk] = (a * jax.nn.sigmoid(a) * u).astype(BF16)
        x2 = x1 + jnp.dot(hid_ref[rows, :], wfd_ref[...], preferred_element_type=F32)

        h3 = _rms(x2, lnp_ref[...]).astype(BF16)
        g_ple = jax.nn.sigmoid(jnp.dot(h3, wpg_ref[...], preferred_element_type=F32))
        ple = jnp.dot(p_ref[rows, :].astype(BF16), wpp_ref[...], preferred_element_type=F32)
        x3 = x2 + g_ple * ple
        o_ref[rows, :] = _rms(x3, fin_ref[...])


def _out_ffn(x, o_sb, o_mb, gates, p, w_bs, w_bm, w_out, ln_ffn, w_fg, w_fu, w_fd, ln_ple, w_pg, w_pp,
             fin_g, tm):
    n, d = x.shape
    row = lambda width: pl.BlockSpec((tm, width), lambda i: (i, 0))

    def const(arr):
        return pl.BlockSpec(arr.shape, lambda i: (0, 0), pipeline_mode=pl.Buffered(1))

    weights = (w_bs, w_bm, w_out, ln_ffn, w_fg, w_fu, w_fd, ln_ple, w_pg, w_pp, fin_g)
    return pl.pallas_call(
        _out_ffn_kernel,
        grid=(n // tm,),
        in_specs=[row(d), row(o_sb.shape[1]), row(o_mb.shape[1]), row(gates.shape[1]), row(p.shape[1])]
        + [const(w) for w in weights],
        out_specs=row(d),
        out_shape=jax.ShapeDtypeStruct((n, d), F32),
        scratch_shapes=[pltpu.VMEM((tm, w_fg.shape[1]), BF16)],
        compiler_params=pltpu.CompilerParams(
            dimension_semantics=("arbitrary",), vmem_limit_bytes=VMEM_LIMIT_BYTES),
        name="out_ffn",
    )(x, o_sb, o_mb, gates, p, *weights)


def _t5_bucket_table(n):
    dist = np.arange(n)
    nf = np.maximum(dist, 1).astype(np.float32)
    large = MAX_EXACT + (np.log(nf / np.float32(MAX_EXACT)) / np.float32(math.log(MAX_DISTANCE / MAX_EXACT))
                         * np.float32(NUM_BUCKETS - MAX_EXACT)).astype(np.int32)
    large = np.minimum(large, NUM_BUCKETS - 1)
    return np.where(dist < MAX_EXACT, dist, large).astype(np.int32)


def _bucket_tiles():
    table = _t5_bucket_table(2 * BLK)
    s_idx = np.arange(BLK)[:, None]
    t_idx = np.arange(BLK)[None, :]
    own = table[np.maximum(t_idx - s_idx, 0)]
    prev = table[t_idx - s_idx + BLK]
    return np.stack([own, prev]).astype(np.int32)


def _suffix_matrix():
    t = (np.arange(HALF)[None, :] >= np.arange(HALF)[:, None]).astype(np.float32)
    return np.concatenate([t, t], axis=1)


def kernel(x, p, ln_mix_g, w_in, w_gate, b_gate, w_branch_sb, w_branch_moba, w_out, rel_bias, ln_ffn_g,
           w_ffn_gate, w_ffn_up, w_ffn_down, ln_ple_g, w_ple_gate, w_ple_proj, final_g):
    depth = w_in.shape[0]
    assert depth == 1, "the final RMSNorm is fused into the last layer's kernel"
    b, s, d = x.shape
    assert s % BLK == 0
    scale = HEAD_DIM ** -0.5
    sb_scale = scale * LOG2_E
    tri = jnp.asarray(_suffix_matrix(), BF16)
    buckets = jnp.asarray(_bucket_tiles())
    tm_in = min(512, s)
    tm_out = min(512, b * s)
    for i in range(depth):
        w = w_in[i]
        q_sb, k_sb, v_sb, q_mb, k_mb, v_mb = (w[:, j * MIX_WIDTH:(j + 1) * MIX_WIDTH] for j in range(6))
        w_qk = jnp.concatenate([q_sb * sb_scale, k_sb, q_mb * sb_scale, k_mb], axis=1).astype(BF16)
        w_vt = jnp.concatenate([v_sb, v_mb], axis=1).T.astype(BF16)
        qk, vt, gates = _in_proj(x, ln_mix_g[i][None], w_qk, w_vt, w_gate[i].astype(BF16), b_gate[i][None], tm_in)
        o_sb = _sb_attention(qk, vt, tri)
        o_mb = _moba_attention(qk, vt, rel_bias, buckets)
        x = _out_ffn(
            x.reshape(b * s, d), o_sb.reshape(b * s, MIX_WIDTH), o_mb.reshape(b * s, MIX_WIDTH),
            gates.reshape(b * s, 2 * d), p[i].reshape(b * s, -1),
            w_branch_sb[i].astype(BF16), w_branch_moba[i].astype(BF16), w_out[i].astype(BF16),
            ln_ffn_g[i][None], w_ffn_gate[i].astype(BF16), w_ffn_up[i].astype(BF16),
            w_ffn_down[i].astype(BF16), ln_ple_g[i][None], w_ple_gate[i].astype(BF16),
            w_ple_proj[i].astype(BF16), final_g[None], tm_out).reshape(b, s, d)
    return x
```

```python
import functools
import math

import jax
import jax.numpy as jnp
import numpy as np
from jax import lax
from jax.experimental import pallas as pl
from jax.experimental.pallas import tpu as pltpu

HEAD_DIM = 64
N_HEADS = 8
MIX_WIDTH = N_HEADS * HEAD_DIM
HEADS_PER_GROUP = 4
GROUP_WIDTH = HEADS_PER_GROUP * HEAD_DIM
N_GROUPS = N_HEADS // HEADS_PER_GROUP
BLK = 256
MOBA_TOPK = 3
NUM_BUCKETS = 32
MAX_EXACT = NUM_BUCKETS // 2
MAX_DISTANCE = 128
RMS_EPS = 1e-6
HALF = BLK // 2
LOG2_E = 1.0 / math.log(2.0)
MOBA_SKEW = 1
ROW_SUBTILE = 256
NEAR_BLOCKS = 2
SB_SKIP_BITS = 160.0
NEG = -1e30
VMEM_LIMIT_BYTES = 56 * 1024 * 1024

F32 = jnp.float32
BF16 = jnp.bfloat16
NT_DIMS = (((1,), (1,)), ((), ()))


def _rms(x, g):
    return x * lax.rsqrt(jnp.mean(x * x, axis=-1, keepdims=True) + RMS_EPS) * g


def _in_proj_kernel(x_ref, g_ref, wqk_ref, wvt_ref, wg_ref, bg_ref, qk_ref, vt_ref, gate_ref):
    subs = [slice(r, r + ROW_SUBTILE) for r in range(0, x_ref.shape[0], ROW_SUBTILE)]
    h = [_rms(x_ref[rows, :], g_ref[...]).astype(BF16) for rows in subs]
    for rows, hs in zip(subs, h):
        qk_ref[rows, :] = jnp.dot(hs, wqk_ref[...], preferred_element_type=F32).astype(BF16)
    for rows, hs in zip(subs, h):
        vt_ref[:, rows] = lax.dot_general(wvt_ref[...], hs, NT_DIMS, preferred_element_type=F32).astype(BF16)
    for rows, hs in zip(subs, h):
        gl = jnp.dot(hs, wg_ref[...], preferred_element_type=F32) + bg_ref[...]
        gate_ref[rows, :] = jax.nn.sigmoid(gl).astype(BF16)


def _in_proj(x, ln_g, w_qk, w_vt, w_gate, b_gate, tm):
    b, s, d = x.shape
    n_qk, n_v, n_g = w_qk.shape[1], w_vt.shape[0], w_gate.shape[1]

    def const(shape):
        return pl.BlockSpec(shape, lambda bi, ti: (0, 0), pipeline_mode=pl.Buffered(1))

    return pl.pallas_call(
        _in_proj_kernel,
        grid=(b, s // tm),
        in_specs=[
            pl.BlockSpec((None, tm, d), lambda bi, ti: (bi, ti, 0)),
            const((1, d)),
            const((d, n_qk)),
            const((n_v, d)),
            const((d, n_g)),
            const((1, n_g)),
        ],
        out_specs=[
            pl.BlockSpec((None, tm, n_qk), lambda bi, ti: (bi, ti, 0)),
            pl.BlockSpec((None, n_v, tm), lambda bi, ti: (bi, 0, ti)),
            pl.BlockSpec((None, tm, n_g), lambda bi, ti: (bi, ti, 0)),
        ],
        out_shape=[
            jax.ShapeDtypeStruct((b, s, n_qk), BF16),
            jax.ShapeDtypeStruct((b, n_v, s), BF16),
            jax.ShapeDtypeStruct((b, s, n_g), BF16),
        ],
        compiler_params=pltpu.CompilerParams(
            dimension_semantics=("arbitrary", "arbitrary"), vmem_limit_bytes=VMEM_LIMIT_BYTES),
        name="in_proj",
    )(x, ln_g, w_qk, w_vt, w_gate, b_gate)


def _sb_kernel(q_ref, k_ref, vt_ref, tri_ref, o_ref, acc_ref, carry_ref, qm_ref, more_ref):
    n_blk = q_ref.shape[0] // BLK
    tri = tri_ref[...]
    lane_head = lax.broadcasted_iota(jnp.int32, (BLK, GROUP_WIDTH), 1) // HEAD_DIM
    heads = range(HEADS_PER_GROUP)
    aligned = lambda x: x if isinstance(x, int) else pl.multiple_of(x, BLK)

    def masked_queries(i):
        qb = q_ref[pl.ds(aligned(i * BLK), BLK), :]
        return [jnp.where(lane_head == h, qb, jnp.zeros_like(qb)) for h in heads]

    half = lambda x, c: x[c * HALF:(c + 1) * HALF]

    class Job:
        def __init__(self, h, q_masked, k0, n_kb, carry, diagonal):
            self.h, self.q, self.k0, self.n_kb, self.carry, self.diagonal = h, q_masked, k0, n_kb, carry, diagonal

        def mask(self, x):
            n_keys = self.n_kb * BLK
            s_idx = lax.broadcasted_iota(jnp.int32, (BLK, BLK), 0)
            t_idx = lax.broadcasted_iota(jnp.int32, (BLK, BLK), 1)
            last = jnp.where(s_idx < t_idx, x[n_keys - BLK:], 0.0)
            return last if self.n_kb == 1 else jnp.concatenate([x[:n_keys - BLK], last], axis=0)

    def stage_scores(job):
        keys = k_ref[pl.ds(job.k0, job.n_kb * BLK), :]
        job.z = lax.dot_general(keys, job.q, NT_DIMS, preferred_element_type=F32)

    def stage_split(job):
        a = jnp.maximum(job.z, 0.0) + jnp.log(1.0 + jnp.exp2(-jnp.abs(job.z))) * LOG2_E
        if job.diagonal:
            a = job.mask(a)
        job.hi = a.astype(BF16)
        job.lo = (a - job.hi.astype(F32)).astype(BF16)

    def stage_suffix(job):
        job.inc = [jnp.dot(tri, jnp.concatenate([half(job.hi, c), half(job.lo, c)], axis=0),
                           preferred_element_type=F32) for c in range(2 * job.n_kb)]

    def stage_weights(job):
        n_half = 2 * job.n_kb
        row, args = job.carry, [None] * n_half
        for c in reversed(range(n_half)):
            args[c] = half(job.z, c) - job.inc[c] - row
            row = row + job.inc[c][0:1, :]
        w = jnp.exp2(jnp.concatenate(args, axis=0))
        job.w = (job.mask(w) if job.diagonal else w).astype(BF16)
        job.carry = row

    def stage_pv(job):
        rows = slice(job.h * HEAD_DIM, (job.h + 1) * HEAD_DIM)
        job.pv = jnp.dot(vt_ref[rows, pl.ds(job.k0, job.n_kb * BLK)], job.w, preferred_element_type=F32)

    stages = (stage_scores, stage_split, stage_suffix, stage_weights, stage_pv)

    def live(carry):
        return jnp.min(jnp.concatenate(carry, axis=0)) < SB_SKIP_BITS

    jobs = []
    for i in range(n_blk):
        n_near = min(i + 1, NEAR_BLOCKS)
        qm = masked_queries(i)
        jobs += [Job(h, qm[h], (i + 1 - n_near) * BLK, n_near, jnp.zeros((1, BLK), F32), True) for h in heads]
    for step in range(len(jobs) + len(stages) - 1):
        for s, stage in enumerate(stages):
            if 0 <= step - s < len(jobs):
                stage(jobs[step - s])
        done = step - (len(stages) - 1)
        if done >= 0 and done % HEADS_PER_GROUP == HEADS_PER_GROUP - 1:
            i = done // HEADS_PER_GROUP
            block = jobs[done - HEADS_PER_GROUP + 1:done + 1]
            pv = jnp.concatenate([job.pv for job in block], axis=0)
            o_ref[i * BLK:(i + 1) * BLK, :] = pv.T.astype(BF16)
            if i + 1 > NEAR_BLOCKS:
                carry = [job.carry for job in block]
                acc_ref[i] = pv
                carry_ref[i] = jnp.concatenate(carry, axis=0)
                more_ref[i] = live(carry).astype(jnp.int32)

    def far_blocks(i, _):
        @pl.when(more_ref[i] != 0)
        def _():
            first = i + 1 - NEAR_BLOCKS
            qm = masked_queries(i)
            for h in heads:
                qm_ref[h] = qm[h]
            c0 = carry_ref[i]

            def cond(state):
                jj, go, _ = state
                return (jj < first) & go

            def body(state):
                jj, _, c = state
                k0 = aligned((first - 1 - jj) * BLK)
                block = [Job(h, qm_ref[h], k0, 1, c[h], False) for h in heads]
                for stage in stages:
                    for job in block:
                        stage(job)
                acc_ref[i] += jnp.concatenate([job.pv for job in block], axis=0)
                c = tuple(job.carry for job in block)
                return jj + 1, live(c), c

            lax.while_loop(cond, body, (0, True, tuple(c0[h:h + 1, :] for h in heads)))
            o_ref[pl.ds(aligned(i * BLK), BLK), :] = acc_ref[i].T.astype(BF16)

        return 0

    lax.fori_loop(NEAR_BLOCKS, n_blk, far_blocks, 0)


def _sb_attention(qk, vt, tri):
    b, s, _ = qk.shape
    return pl.pallas_call(
        _sb_kernel,
        grid=(b, N_GROUPS),
        in_specs=[
            pl.BlockSpec((None, s, GROUP_WIDTH), lambda bi, gi: (bi, 0, gi)),
            pl.BlockSpec((None, s, GROUP_WIDTH), lambda bi, gi: (bi, 0, N_GROUPS + gi)),
            pl.BlockSpec((None, GROUP_WIDTH, s), lambda bi, gi: (bi, gi, 0)),
            pl.BlockSpec((HALF, 2 * HALF), lambda bi, gi: (0, 0)),
        ],
        out_specs=pl.BlockSpec((None, s, GROUP_WIDTH), lambda bi, gi: (bi, 0, gi)),
        out_shape=jax.ShapeDtypeStruct((b, s, MIX_WIDTH), BF16),
        scratch_shapes=[
            pltpu.VMEM((s // BLK, GROUP_WIDTH, BLK), F32),
            pltpu.VMEM((s // BLK, HEADS_PER_GROUP, BLK), F32),
            pltpu.VMEM((HEADS_PER_GROUP, BLK, GROUP_WIDTH), BF16),
            pltpu.SMEM((s // BLK,), jnp.int32),
        ],
        compiler_params=pltpu.CompilerParams(
            dimension_semantics=("arbitrary", "arbitrary"), vmem_limit_bytes=VMEM_LIMIT_BYTES),
        name="sb_attn",
    )(qk, qk, vt, tri)


def _moba_kernel(bias_ref, q_ref, k_ref, vt_ref, bkt_ref, o_ref, btile_ref, madd_ref):
    gi = pl.program_id(0)
    bi = pl.program_id(1)
    s_len = q_ref.shape[0]
    n_blk = s_len // BLK
    far_bucket = NUM_BUCKETS - 1

    @pl.when(bi == 0)
    def _():
        for which in range(2):
            bkt = bkt_ref[which]
            for h in range(HEADS_PER_GROUP):
                t = jnp.zeros((BLK, BLK), F32)
                for bu in range(NUM_BUCKETS):
                    t = jnp.where(bkt == bu, bias_ref[bu, gi * HEADS_PER_GROUP + h] * LOG2_E, t)
                btile_ref[which, h] = t

    kbar = jnp.concatenate(
        [jnp.sum(k_ref[n * BLK:(n + 1) * BLK, :].astype(F32), axis=0, keepdims=True) for n in range(n_blk)],
        axis=0) * (1.0 / BLK)
    lane_head = lax.broadcasted_iota(jnp.int32, kbar.shape, 1) // HEAD_DIM
    pieces, rem = [], kbar
    for _ in range(3):
        p = rem.astype(BF16).astype(F32)
        pieces.append(p)
        rem = rem - p
    rows = [jnp.where(lane_head == h, p, 0.0) for p in pieces for h in range(HEADS_PER_GROUP)]
    r = jnp.concatenate(rows, axis=0).astype(BF16)
    gt = lax.dot_general(r, q_ref[...], NT_DIMS, preferred_element_type=F32)
    npc = HEADS_PER_GROUP * n_blk
    gate = gt[0:npc] + gt[npc:2 * npc] + gt[2 * npc:3 * npc]
    blk_idx = lax.broadcasted_iota(jnp.int32, (n_blk, s_len), 0)
    cur_blk = lax.broadcasted_iota(jnp.int32, (n_blk, s_len), 1) // BLK
    past = blk_idx < cur_blk
    for h in range(HEADS_PER_GROUP):
        g = jnp.where(past, gate[h * n_blk:(h + 1) * n_blk], -jnp.inf)
        beaten_by = jnp.zeros((n_blk, s_len), jnp.int32)
        for m in range(n_blk):
            gm = g[m:m + 1, :]
            beats = (gm > g) | ((gm == g) & (m < blk_idx))
            beaten_by = beaten_by + beats.astype(jnp.int32)
        selected = (beaten_by < MOBA_TOPK) & past
        madd_ref[h] = jnp.where(selected, 0.0, NEG)

    s_idx = lax.broadcasted_iota(jnp.int32, (BLK, BLK), 0)
    t_idx = lax.broadcasted_iota(jnp.int32, (BLK, BLK), 1)
    causal = s_idx <= t_idx
    lane_head_q = lax.broadcasted_iota(jnp.int32, (BLK, GROUP_WIDTH), 1) // HEAD_DIM

    heads = range(HEADS_PER_GROUP)
    far_bias = [bias_ref[far_bucket, gi * HEADS_PER_GROUP + h] * LOG2_E for h in heads]

    state, out_rows = {}, {}

    def stage_scores(i, h):
        qb = q_ref[i * BLK:(i + 1) * BLK, :]
        qh = jnp.where(lane_head_q == h, qb, jnp.zeros_like(qb))
        state[i, h] = lax.dot_general(k_ref[0:(i + 1) * BLK, :], qh, NT_DIMS,
                                      preferred_element_type=F32)

    def stage_max(i, h):
        z = state[i, h]
        madd = madd_ref[h, :, i * BLK:(i + 1) * BLK]
        blocks, adds = [], []
        for j in range(i + 1):
            zj = z[j * BLK:(j + 1) * BLK]
            if j == i:
                blocks.append(jnp.where(causal, zj + btile_ref[0, h], NEG))
                adds.append(None)
            elif j == i - 1:
                blocks.append(zj + btile_ref[1, h])
                adds.append(madd[j:j + 1, :])
            else:
                blocks.append(zj)
                adds.append(madd[j:j + 1, :] + far_bias[h])
        mx = None
        for blk, add in zip(blocks, adds):
            cm = jnp.max(blk, axis=0, keepdims=True)
            cm = cm if add is None else cm + add
            mx = cm if mx is None else jnp.maximum(mx, cm)
        state[i, h] = (blocks, adds, mx)

    def stage_exp(i, h):
        blocks, adds, mx = state[i, h]
        lsum, probs = None, []
        for blk, add in zip(blocks, adds):
            p = jnp.exp2(blk + ((-mx) if add is None else (add - mx)))
            ps = jnp.sum(p, axis=0, keepdims=True)
            lsum = ps if lsum is None else lsum + ps
            probs.append(p.astype(BF16))
        state[i, h] = (jnp.concatenate(probs, axis=0), lsum)

    def stage_pv(i, h):
        probs, lsum = state.pop((i, h))
        rows = slice(h * HEAD_DIM, (h + 1) * HEAD_DIM)
        pv = jnp.dot(vt_ref[rows, 0:(i + 1) * BLK], probs, preferred_element_type=F32)
        out_rows.setdefault(i, []).append(pv * (1.0 / lsum))
        if h == HEADS_PER_GROUP - 1:
            o_ref[i * BLK:(i + 1) * BLK, :] = jnp.concatenate(out_rows.pop(i), axis=0).T.astype(BF16)

    items = [(i, h) for i in range(n_blk) for h in heads]
    stages = (stage_scores, stage_max, stage_exp, stage_pv)
    for step in range(len(items) + (len(stages) - 1) * MOBA_SKEW):
        for s, stage in enumerate(stages):
            if 0 <= step - s * MOBA_SKEW < len(items):
                stage(*items[step - s * MOBA_SKEW])


def _moba_attention(qk, vt, rel_bias, buckets):
    b, s, _ = qk.shape
    n_blk = s // BLK
    return pl.pallas_call(
        _moba_kernel,
        grid=(N_GROUPS, b),
        in_specs=[
            pl.BlockSpec(memory_space=pltpu.SMEM),
            pl.BlockSpec((None, s, GROUP_WIDTH), lambda gi, bi: (bi, 0, 2 * N_GROUPS + gi)),
            pl.BlockSpec((None, s, GROUP_WIDTH), lambda gi, bi: (bi, 0, 3 * N_GROUPS + gi)),
            pl.BlockSpec((None, GROUP_WIDTH, s), lambda gi, bi: (bi, N_GROUPS + gi, 0)),
            pl.BlockSpec((2, BLK, BLK), lambda gi, bi: (0, 0, 0)),
        ],
        out_specs=pl.BlockSpec((None, s, GROUP_WIDTH), lambda gi, bi: (bi, 0, gi)),
        out_shape=jax.ShapeDtypeStruct((b, s, MIX_WIDTH), BF16),
        scratch_shapes=[
            pltpu.VMEM((2, HEADS_PER_GROUP, BLK, BLK), F32),
            pltpu.VMEM((HEADS_PER_GROUP, n_blk, s), F32),
        ],
        compiler_params=pltpu.CompilerParams(
            dimension_semantics=("arbitrary", "arbitrary"), vmem_limit_bytes=VMEM_LIMIT_BYTES),
        name="moba_attn",
    )(rel_bias, qk, qk, vt, buckets)


def _out_ffn_kernel(x_ref, osb_ref, omb_ref, gate_ref, p_ref, wbs_ref, wbm_ref, wo_ref, lnf_ref,
                    wfg_ref, wfu_ref, wfd_ref, lnp_ref, wpg_ref, wpp_ref, fin_ref, o_ref, hid_ref):
    tm, d = x_ref.shape
    ffn = wfg_ref.shape[1]
    chunk = 256
    subs = [slice(r, r + ROW_SUBTILE) for r in range(0, tm, ROW_SUBTILE)]
    y = [(jnp.dot(osb_ref[rows, :], wbs_ref[...], preferred_element_type=F32),
          jnp.dot(omb_ref[rows, :], wbm_ref[...], preferred_element_type=F32)) for rows in subs]
    x1 = []
    for rows, (y_sb, y_mb) in zip(subs, y):
        mix = gate_ref[rows, :d].astype(F32) * y_sb + gate_ref[rows, d:].astype(F32) * y_mb
        x1.append(x_ref[rows, :] + jnp.dot(mix.astype(BF16), wo_ref[...], preferred_element_type=F32))
    h2 = [_rms(v, lnf_ref[...]).astype(BF16) for v in x1]
    for c in range(0, ffn, chunk):
        for rows, h in zip(subs, h2):
            a = jnp.dot(h, wfg_ref[:, c:c + chunk], preferred_element_type=F32)
            u = jnp.dot(h, wfu_ref[:, c:c + chunk], preferred_element_type=F32)
            hid_ref[rows, c:c + chunk] = (a * jax.nn.sigmoid(a) * u).astype(BF16)
    x2 = [v + jnp.dot(hid_ref[rows, :], wfd_ref[...], preferred_element_type=F32) for rows, v in zip(subs, x1)]
    h3 = [_rms(v, lnp_ref[...]).astype(BF16) for v in x2]
    for rows, v, h in zip(subs, x2, h3):
        g_ple = jax.nn.sigmoid(jnp.dot(h, wpg_ref[...], preferred_element_type=F32))
        ple = jnp.dot(p_ref[rows, :].astype(BF16), wpp_ref[...], preferred_element_type=F32)
        o_ref[rows, :] = _rms(v + g_ple * ple, fin_ref[...])


def _out_ffn(x, o_sb, o_mb, gates, p, w_bs, w_bm, w_out, ln_ffn, w_fg, w_fu, w_fd, ln_ple, w_pg, w_pp,
             fin_g, tm):
    n, d = x.shape
    row = lambda width: pl.BlockSpec((tm, width), lambda i: (i, 0))

    def const(arr):
        return pl.BlockSpec(arr.shape, lambda i: (0, 0), pipeline_mode=pl.Buffered(1))

    weights = (w_bs, w_bm, w_out, ln_ffn, w_fg, w_fu, w_fd, ln_ple, w_pg, w_pp, fin_g)
    return pl.pallas_call(
        _out_ffn_kernel,
        grid=(n // tm,),
        in_specs=[row(d), row(o_sb.shape[1]), row(o_mb.shape[1]), row(gates.shape[1]), row(p.shape[1])]
        + [const(w) for w in weights],
        out_specs=row(d),
        out_shape=jax.ShapeDtypeStruct((n, d), F32),
        scratch_shapes=[pltpu.VMEM((tm, w_fg.shape[1]), BF16)],
        compiler_params=pltpu.CompilerParams(
            dimension_semantics=("arbitrary",), vmem_limit_bytes=VMEM_LIMIT_BYTES),
        name="out_ffn",
    )(x, o_sb, o_mb, gates, p, *weights)


def _t5_bucket_table(n):
    dist = np.arange(n)
    nf = np.maximum(dist, 1).astype(np.float32)
    large = MAX_EXACT + (np.log(nf / np.float32(MAX_EXACT)) / np.float32(math.log(MAX_DISTANCE / MAX_EXACT))
                         * np.float32(NUM_BUCKETS - MAX_EXACT)).astype(np.int32)
    large = np.minimum(large, NUM_BUCKETS - 1)
    return np.where(dist < MAX_EXACT, dist, large).astype(np.int32)


def _bucket_tiles():
    table = _t5_bucket_table(2 * BLK)
    s_idx = np.arange(BLK)[:, None]
    t_idx = np.arange(BLK)[None, :]
    own = table[np.maximum(t_idx - s_idx, 0)]
    prev = table[t_idx - s_idx + BLK]
    return np.stack([own, prev]).astype(np.int32)


def _suffix_matrix():
    t = (np.arange(HALF)[None, :] >= np.arange(HALF)[:, None]).astype(np.float32)
    return np.concatenate([t, t], axis=1)


def kernel(x, p, ln_mix_g, w_in, w_gate, b_gate, w_branch_sb, w_branch_moba, w_out, rel_bias, ln_ffn_g,
           w_ffn_gate, w_ffn_up, w_ffn_down, ln_ple_g, w_ple_gate, w_ple_proj, final_g):
    depth = w_in.shape[0]
    assert depth == 1, "the final RMSNorm is fused into the last layer's kernel"
    b, s, d = x.shape
    assert s % BLK == 0
    scale = HEAD_DIM ** -0.5
    sb_scale = scale * LOG2_E
    tri = jnp.asarray(_suffix_matrix(), BF16)
    buckets = jnp.asarray(_bucket_tiles())
    tm_in = min(512, s)
    tm_out = min(512, b * s)
    for i in range(depth):
        w = w_in[i]
        q_sb, k_sb, v_sb, q_mb, k_mb, v_mb = (w[:, j * MIX_WIDTH:(j + 1) * MIX_WIDTH] for j in range(6))
        w_qk = jnp.concatenate([q_sb * sb_scale, k_sb, q_mb * sb_scale, k_mb], axis=1).astype(BF16)
        w_vt = jnp.concatenate([v_sb, v_mb], axis=1).T.astype(BF16)
        qk, vt, gates = _in_proj(x, ln_mix_g[i][None], w_qk, w_vt, w_gate[i].astype(BF16), b_gate[i][None], tm_in)
        o_sb = _sb_attention(qk, vt, tri)
        o_mb = _moba_attention(qk, vt, rel_bias, buckets)
        x = _out_ffn(
            x.reshape(b * s, d), o_sb.reshape(b * s, MIX_WIDTH), o_mb.reshape(b * s, MIX_WIDTH),
            gates.reshape(b * s, 2 * d), p[i].reshape(b * s, -1),
            w_branch_sb[i].astype(BF16), w_branch_moba[i].astype(BF16), w_out[i].astype(BF16),
            ln_ffn_g[i][None], w_ffn_gate[i].astype(BF16), w_ffn_up[i].astype(BF16),
            w_ffn_down[i].astype(BF16), ln_ple_g[i][None], w_ple_gate[i].astype(BF16),
            w_ple_proj[i].astype(BF16), final_g[None], tm_out).reshape(b, s, d)
    return x
```

```python
import functools
import math

import jax
import jax.numpy as jnp
import numpy as np
from jax import lax
from jax.experimental import pallas as pl
from jax.experimental.pallas import tpu as pltpu

HEAD_DIM = 64
N_HEADS = 8
MIX_WIDTH = N_HEADS * HEAD_DIM
HEADS_PER_GROUP = 4
GROUP_WIDTH = HEADS_PER_GROUP * HEAD_DIM
N_GROUPS = N_HEADS // HEADS_PER_GROUP
BLK = 256
MOBA_TOPK = 3
NUM_BUCKETS = 32
MAX_EXACT = NUM_BUCKETS // 2
MAX_DISTANCE = 128
RMS_EPS = 1e-6
HALF = BLK // 2
LOG2_E = 1.0 / math.log(2.0)
ONES_ROWS = 16
MOBA_SKEW = 1
ROW_SUBTILE = 256
NEAR_BLOCKS = 2
SB_SKIP_BITS = 160.0
NEG = -1e30
VMEM_LIMIT_BYTES = 56 * 1024 * 1024

F32 = jnp.float32
BF16 = jnp.bfloat16
NT_DIMS = (((1,), (1,)), ((), ()))


def _rms(x, g):
    return x * lax.rsqrt(jnp.mean(x * x, axis=-1, keepdims=True) + RMS_EPS) * g


def _in_proj_kernel(x_ref, g_ref, wqk_ref, wvt_ref, wg_ref, bg_ref, qk_ref, vt_ref, gate_ref):
    subs = [slice(r, r + ROW_SUBTILE) for r in range(0, x_ref.shape[0], ROW_SUBTILE)]
    h = [_rms(x_ref[rows, :], g_ref[...]).astype(BF16) for rows in subs]
    for rows, hs in zip(subs, h):
        qk_ref[rows, :] = jnp.dot(hs, wqk_ref[...], preferred_element_type=F32).astype(BF16)
    for rows, hs in zip(subs, h):
        vt_ref[:, rows] = lax.dot_general(wvt_ref[...], hs, NT_DIMS, preferred_element_type=F32).astype(BF16)
    for rows, hs in zip(subs, h):
        gl = jnp.dot(hs, wg_ref[...], preferred_element_type=F32) + bg_ref[...]
        gate_ref[rows, :] = jax.nn.sigmoid(gl).astype(BF16)


def _in_proj(x, ln_g, w_qk, w_vt, w_gate, b_gate, tm):
    b, s, d = x.shape
    n_qk, n_v, n_g = w_qk.shape[1], w_vt.shape[0], w_gate.shape[1]

    def const(shape):
        return pl.BlockSpec(shape, lambda bi, ti: (0, 0), pipeline_mode=pl.Buffered(1))

    return pl.pallas_call(
        _in_proj_kernel,
        grid=(b, s // tm),
        in_specs=[
            pl.BlockSpec((None, tm, d), lambda bi, ti: (bi, ti, 0)),
            const((1, d)),
            const((d, n_qk)),
            const((n_v, d)),
            const((d, n_g)),
            const((1, n_g)),
        ],
        out_specs=[
            pl.BlockSpec((None, tm, n_qk), lambda bi, ti: (bi, ti, 0)),
            pl.BlockSpec((None, n_v, tm), lambda bi, ti: (bi, 0, ti)),
            pl.BlockSpec((None, tm, n_g), lambda bi, ti: (bi, ti, 0)),
        ],
        out_shape=[
            jax.ShapeDtypeStruct((b, s, n_qk), BF16),
            jax.ShapeDtypeStruct((b, n_v, s), BF16),
            jax.ShapeDtypeStruct((b, s, n_g), BF16),
        ],
        compiler_params=pltpu.CompilerParams(
            dimension_semantics=("arbitrary", "arbitrary"), vmem_limit_bytes=VMEM_LIMIT_BYTES),
        name="in_proj",
    )(x, ln_g, w_qk, w_vt, w_gate, b_gate)


def _sb_kernel(q_ref, k_ref, vt_ref, tri_ref, o_ref, acc_ref, carry_ref, qm_ref, more_ref):
    n_blk = q_ref.shape[0] // BLK
    tri = tri_ref[...]
    lane_head = lax.broadcasted_iota(jnp.int32, (BLK, GROUP_WIDTH), 1) // HEAD_DIM
    heads = range(HEADS_PER_GROUP)
    aligned = lambda x: x if isinstance(x, int) else pl.multiple_of(x, BLK)

    def masked_queries(i):
        qb = q_ref[pl.ds(aligned(i * BLK), BLK), :]
        return [jnp.where(lane_head == h, qb, jnp.zeros_like(qb)) for h in heads]

    half = lambda x, c: x[c * HALF:(c + 1) * HALF]

    class Job:
        def __init__(self, h, q_masked, k0, n_kb, carry, diagonal):
            self.h, self.q, self.k0, self.n_kb, self.carry, self.diagonal = h, q_masked, k0, n_kb, carry, diagonal

        def mask(self, x):
            n_keys = self.n_kb * BLK
            s_idx = lax.broadcasted_iota(jnp.int32, (BLK, BLK), 0)
            t_idx = lax.broadcasted_iota(jnp.int32, (BLK, BLK), 1)
            last = jnp.where(s_idx < t_idx, x[n_keys - BLK:], 0.0)
            return last if self.n_kb == 1 else jnp.concatenate([x[:n_keys - BLK], last], axis=0)

    def stage_scores(job):
        keys = k_ref[pl.ds(job.k0, job.n_kb * BLK), :]
        job.z = lax.dot_general(keys, job.q, NT_DIMS, preferred_element_type=F32)

    def stage_split(job):
        neg_abs = pltpu.bitcast(pltpu.bitcast(job.z, jnp.int32) | jnp.int32(-2 ** 31), F32)
        a = jnp.maximum(job.z, 0.0) + jnp.log(1.0 + jnp.exp2(neg_abs)) * LOG2_E
        if job.diagonal:
            a = job.mask(a)
        job.hi = a.astype(BF16)
        job.lo = (a - job.hi.astype(F32)).astype(BF16)

    def stage_suffix(job):
        job.inc = [jnp.dot(tri, jnp.concatenate([half(job.hi, c), half(job.lo, c)], axis=0),
                           preferred_element_type=F32) for c in range(2 * job.n_kb)]

    def stage_weights(job):
        n_half = 2 * job.n_kb
        row, args = job.carry, [None] * n_half
        for c in reversed(range(n_half)):
            args[c] = half(job.z, c) - job.inc[c] - row
            row = row + job.inc[c][0:1, :]
        w = jnp.exp2(jnp.concatenate(args, axis=0))
        job.w = (job.mask(w) if job.diagonal else w).astype(BF16)
        job.carry = row

    def stage_pv(job):
        rows = slice(job.h * HEAD_DIM, (job.h + 1) * HEAD_DIM)
        job.pv = jnp.dot(vt_ref[rows, pl.ds(job.k0, job.n_kb * BLK)], job.w, preferred_element_type=F32)

    stages = (stage_scores, stage_split, stage_suffix, stage_weights, stage_pv)

    def live(carry):
        return jnp.min(jnp.concatenate(carry, axis=0)) < SB_SKIP_BITS

    jobs = []
    for i in range(n_blk):
        n_near = min(i + 1, NEAR_BLOCKS)
        qm = masked_queries(i)
        jobs += [Job(h, qm[h], (i + 1 - n_near) * BLK, n_near, jnp.zeros((1, BLK), F32), True) for h in heads]
    for step in range(len(jobs) + len(stages) - 1):
        for s, stage in enumerate(stages):
            if 0 <= step - s < len(jobs):
                stage(jobs[step - s])
        done = step - (len(stages) - 1)
        if done >= 0 and done % HEADS_PER_GROUP == HEADS_PER_GROUP - 1:
            i = done // HEADS_PER_GROUP
            block = jobs[done - HEADS_PER_GROUP + 1:done + 1]
            pv = jnp.concatenate([job.pv for job in block], axis=0)
            o_ref[i * BLK:(i + 1) * BLK, :] = pv.T.astype(BF16)
            if i + 1 > NEAR_BLOCKS:
                carry = [job.carry for job in block]
                acc_ref[i] = pv
                carry_ref[i] = jnp.concatenate(carry, axis=0)
                more_ref[i] = live(carry).astype(jnp.int32)

    def far_blocks(i, _):
        @pl.when(more_ref[i] != 0)
        def _():
            first = i + 1 - NEAR_BLOCKS
            qm = masked_queries(i)
            for h in heads:
                qm_ref[h] = qm[h]
            c0 = carry_ref[i]

            def cond(state):
                jj, go, _ = state
                return (jj < first) & go

            def body(state):
                jj, _, c = state
                k0 = aligned((first - 1 - jj) * BLK)
                block = [Job(h, qm_ref[h], k0, 1, c[h], False) for h in heads]
                for stage in stages:
                    for job in block:
                        stage(job)
                acc_ref[i] += jnp.concatenate([job.pv for job in block], axis=0)
                c = tuple(job.carry for job in block)
                return jj + 1, live(c), c

            lax.while_loop(cond, body, (0, True, tuple(c0[h:h + 1, :] for h in heads)))
            o_ref[pl.ds(aligned(i * BLK), BLK), :] = acc_ref[i].T.astype(BF16)

        return 0

    lax.fori_loop(NEAR_BLOCKS, n_blk, far_blocks, 0)


def _sb_attention(qk, vt, tri):
    b, s, _ = qk.shape
    return pl.pallas_call(
        _sb_kernel,
        grid=(b, N_GROUPS),
        in_specs=[
            pl.BlockSpec((None, s, GROUP_WIDTH), lambda bi, gi: (bi, 0, gi)),
            pl.BlockSpec((None, s, GROUP_WIDTH), lambda bi, gi: (bi, 0, N_GROUPS + gi)),
            pl.BlockSpec((None, GROUP_WIDTH, s), lambda bi, gi: (bi, gi, 0)),
            pl.BlockSpec((HALF, 2 * HALF), lambda bi, gi: (0, 0)),
        ],
        out_specs=pl.BlockSpec((None, s, GROUP_WIDTH), lambda bi, gi: (bi, 0, gi)),
        out_shape=jax.ShapeDtypeStruct((b, s, MIX_WIDTH), BF16),
        scratch_shapes=[
            pltpu.VMEM((s // BLK, GROUP_WIDTH, BLK), F32),
            pltpu.VMEM((s // BLK, HEADS_PER_GROUP, BLK), F32),
            pltpu.VMEM((HEADS_PER_GROUP, BLK, GROUP_WIDTH), BF16),
            pltpu.SMEM((s // BLK,), jnp.int32),
        ],
        compiler_params=pltpu.CompilerParams(
            dimension_semantics=("arbitrary", "arbitrary"), vmem_limit_bytes=VMEM_LIMIT_BYTES),
        name="sb_attn",
    )(qk, qk, vt, tri)


def _moba_kernel(bias_ref, q_ref, k_ref, vt_ref, bkt_ref, o_ref, btile_ref, madd_ref):
    gi = pl.program_id(0)
    bi = pl.program_id(1)
    s_len = q_ref.shape[0]
    n_blk = s_len // BLK
    far_bucket = NUM_BUCKETS - 1

    @pl.when(bi == 0)
    def _():
        for which in range(2):
            bkt = bkt_ref[which]
            for h in range(HEADS_PER_GROUP):
                t = jnp.zeros((BLK, BLK), F32)
                for bu in range(NUM_BUCKETS):
                    t = jnp.where(bkt == bu, bias_ref[bu, gi * HEADS_PER_GROUP + h] * LOG2_E, t)
                btile_ref[which, h] = t

    kbar = jnp.concatenate(
        [jnp.sum(k_ref[n * BLK:(n + 1) * BLK, :].astype(F32), axis=0, keepdims=True) for n in range(n_blk)],
        axis=0) * (1.0 / BLK)
    lane_head = lax.broadcasted_iota(jnp.int32, kbar.shape, 1) // HEAD_DIM
    pieces, rem = [], kbar
    for _ in range(3):
        p = rem.astype(BF16).astype(F32)
        pieces.append(p)
        rem = rem - p
    rows = [jnp.where(lane_head == h, p, 0.0) for p in pieces for h in range(HEADS_PER_GROUP)]
    r = jnp.concatenate(rows, axis=0).astype(BF16)
    gt = lax.dot_general(r, q_ref[...], NT_DIMS, preferred_element_type=F32)
    npc = HEADS_PER_GROUP * n_blk
    gate = gt[0:npc] + gt[npc:2 * npc] + gt[2 * npc:3 * npc]
    blk_idx = lax.broadcasted_iota(jnp.int32, (n_blk, s_len), 0)
    cur_blk = lax.broadcasted_iota(jnp.int32, (n_blk, s_len), 1) // BLK
    past = blk_idx < cur_blk
    for h in range(HEADS_PER_GROUP):
        g = jnp.where(past, gate[h * n_blk:(h + 1) * n_blk], -jnp.inf)
        beaten_by = jnp.zeros((n_blk, s_len), jnp.int32)
        for m in range(n_blk):
            gm = g[m:m + 1, :]
            beats = (gm > g) | ((gm == g) & (m < blk_idx))
            beaten_by = beaten_by + beats.astype(jnp.int32)
        selected = (beaten_by < MOBA_TOPK) & past
        madd_ref[h] = jnp.where(selected, 0.0, NEG)

    s_idx = lax.broadcasted_iota(jnp.int32, (BLK, BLK), 0)
    t_idx = lax.broadcasted_iota(jnp.int32, (BLK, BLK), 1)
    causal = s_idx <= t_idx
    lane_head_q = lax.broadcasted_iota(jnp.int32, (BLK, GROUP_WIDTH), 1) // HEAD_DIM

    heads = range(HEADS_PER_GROUP)
    far_bias = [bias_ref[far_bucket, gi * HEADS_PER_GROUP + h] * LOG2_E for h in heads]

    state, out_rows = {}, {}

    def stage_scores(i, h):
        qb = q_ref[i * BLK:(i + 1) * BLK, :]
        qh = jnp.where(lane_head_q == h, qb, jnp.zeros_like(qb))
        state[i, h] = lax.dot_general(k_ref[0:(i + 1) * BLK, :], qh, NT_DIMS,
                                      preferred_element_type=F32)

    def stage_max(i, h):
        z = state[i, h]
        madd = madd_ref[h, :, i * BLK:(i + 1) * BLK]
        blocks, adds = [], []
        for j in range(i + 1):
            zj = z[j * BLK:(j + 1) * BLK]
            if j == i:
                blocks.append(jnp.where(causal, zj + btile_ref[0, h], NEG))
                adds.append(None)
            elif j == i - 1:
                blocks.append(zj + btile_ref[1, h])
                adds.append(madd[j:j + 1, :])
            else:
                blocks.append(zj)
                adds.append(madd[j:j + 1, :] + far_bias[h])
        mx = None
        for blk, add in zip(blocks, adds):
            cm = jnp.max(blk, axis=0, keepdims=True)
            cm = cm if add is None else cm + add
            mx = cm if mx is None else jnp.maximum(mx, cm)
        state[i, h] = (blocks, adds, mx)

    def stage_exp(i, h):
        blocks, adds, mx = state[i, h]
        probs = [jnp.exp2(blk + ((-mx) if add is None else (add - mx))).astype(BF16)
                 for blk, add in zip(blocks, adds)]
        state[i, h] = jnp.concatenate(probs, axis=0)

    def stage_pv(i, h):
        probs = state.pop((i, h))
        rows = slice(h * HEAD_DIM, (h + 1) * HEAD_DIM)
        n_keys = (i + 1) * BLK
        vt_ones = jnp.concatenate([vt_ref[rows, 0:n_keys], jnp.ones((ONES_ROWS, n_keys), BF16)], axis=0)
        pv = jnp.dot(vt_ones, probs, preferred_element_type=F32)
        lsum = pv[HEAD_DIM:HEAD_DIM + 1, :]
        out_rows.setdefault(i, []).append(pv[:HEAD_DIM] * (1.0 / lsum))
        if h == HEADS_PER_GROUP - 1:
            o_ref[i * BLK:(i + 1) * BLK, :] = jnp.concatenate(out_rows.pop(i), axis=0).T.astype(BF16)

    items = [(i, h) for i in range(n_blk) for h in heads]
    stages = (stage_scores, stage_max, stage_exp, stage_pv)
    for step in range(len(items) + (len(stages) - 1) * MOBA_SKEW):
        for s, stage in enumerate(stages):
            if 0 <= step - s * MOBA_SKEW < len(items):
                stage(*items[step - s * MOBA_SKEW])


def _moba_attention(qk, vt, rel_bias, buckets):
    b, s, _ = qk.shape
    n_blk = s // BLK
    return pl.pallas_call(
        _moba_kernel,
        grid=(N_GROUPS, b),
        in_specs=[
            pl.BlockSpec(memory_space=pltpu.SMEM),
            pl.BlockSpec((None, s, GROUP_WIDTH), lambda gi, bi: (bi, 0, 2 * N_GROUPS + gi)),
            pl.BlockSpec((None, s, GROUP_WIDTH), lambda gi, bi: (bi, 0, 3 * N_GROUPS + gi)),
            pl.BlockSpec((None, GROUP_WIDTH, s), lambda gi, bi: (bi, N_GROUPS + gi, 0)),
            pl.BlockSpec((2, BLK, BLK), lambda gi, bi: (0, 0, 0)),
        ],
        out_specs=pl.BlockSpec((None, s, GROUP_WIDTH), lambda gi, bi: (bi, 0, gi)),
        out_shape=jax.ShapeDtypeStruct((b, s, MIX_WIDTH), BF16),
        scratch_shapes=[
            pltpu.VMEM((2, HEADS_PER_GROUP, BLK, BLK), F32),
            pltpu.VMEM((HEADS_PER_GROUP, n_blk, s), F32),
        ],
        compiler_params=pltpu.CompilerParams(
            dimension_semantics=("arbitrary", "arbitrary"), vmem_limit_bytes=VMEM_LIMIT_BYTES),
        name="moba_attn",
    )(rel_bias, qk, qk, vt, buckets)


def _out_ffn_kernel(x_ref, osb_ref, omb_ref, gate_ref, p_ref, wbs_ref, wbm_ref, wo_ref, lnf_ref,
                    wfg_ref, wfu_ref, wfd_ref, lnp_ref, wpg_ref, wpp_ref, fin_ref, o_ref, hid_ref):
    tm, d = x_ref.shape
    ffn = wfg_ref.shape[1]
    chunk = 256
    subs = [slice(r, r + ROW_SUBTILE) for r in range(0, tm, ROW_SUBTILE)]
    y = [(jnp.dot(osb_ref[rows, :], wbs_ref[...], preferred_element_type=F32),
          jnp.dot(omb_ref[rows, :], wbm_ref[...], preferred_element_type=F32)) for rows in subs]
    x1 = []
    for rows, (y_sb, y_mb) in zip(subs, y):
        mix = gate_ref[rows, :d].astype(F32) * y_sb + gate_ref[rows, d:].astype(F32) * y_mb
        x1.append(x_ref[rows, :] + jnp.dot(mix.astype(BF16), wo_ref[...], preferred_element_type=F32))
    h2 = [_rms(v, lnf_ref[...]).astype(BF16) for v in x1]
    for c in range(0, ffn, chunk):
        for rows, h in zip(subs, h2):
            a = jnp.dot(h, wfg_ref[:, c:c + chunk], preferred_element_type=F32)
            u = jnp.dot(h, wfu_ref[:, c:c + chunk], preferred_element_type=F32)
            hid_ref[rows, c:c + chunk] = (a * jax.nn.sigmoid(a) * u).astype(BF16)
    x2 = [v + jnp.dot(hid_ref[rows, :], wfd_ref[...], preferred_element_type=F32) for rows, v in zip(subs, x1)]
    h3 = [_rms(v, lnp_ref[...]).astype(BF16) for v in x2]
    for rows, v, h in zip(subs, x2, h3):
        g_ple = jax.nn.sigmoid(jnp.dot(h, wpg_ref[...], preferred_element_type=F32))
        ple = jnp.dot(p_ref[rows, :].astype(BF16), wpp_ref[...], preferred_element_type=F32)
        o_ref[rows, :] = _rms(v + g_ple * ple, fin_ref[...])


def _out_ffn(x, o_sb, o_mb, gates, p, w_bs, w_bm, w_out, ln_ffn, w_fg, w_fu, w_fd, ln_ple, w_pg, w_pp,
             fin_g, tm):
    n, d = x.shape
    row = lambda width: pl.BlockSpec((tm, width), lambda i: (i, 0))

    def const(arr):
        return pl.BlockSpec(arr.shape, lambda i: (0, 0), pipeline_mode=pl.Buffered(1))

    weights = (w_bs, w_bm, w_out, ln_ffn, w_fg, w_fu, w_fd, ln_ple, w_pg, w_pp, fin_g)
    return pl.pallas_call(
        _out_ffn_kernel,
        grid=(n // tm,),
        in_specs=[row(d), row(o_sb.shape[1]), row(o_mb.shape[1]), row(gates.shape[1]), row(p.shape[1])]
        + [const(w) for w in weights],
        out_specs=row(d),
        out_shape=jax.ShapeDtypeStruct((n, d), F32),
        scratch_shapes=[pltpu.VMEM((tm, w_fg.shape[1]), BF16)],
        compiler_params=pltpu.CompilerParams(
            dimension_semantics=("arbitrary",), vmem_limit_bytes=VMEM_LIMIT_BYTES),
        name="out_ffn",
    )(x, o_sb, o_mb, gates, p, *weights)


def _t5_bucket_table(n):
    dist = np.arange(n)
    nf = np.maximum(dist, 1).astype(np.float32)
    large = MAX_EXACT + (np.log(nf / np.float32(MAX_EXACT)) / np.float32(math.log(MAX_DISTANCE / MAX_EXACT))
                         * np.float32(NUM_BUCKETS - MAX_EXACT)).astype(np.int32)
    large = np.minimum(large, NUM_BUCKETS - 1)
    return np.where(dist < MAX_EXACT, dist, large).astype(np.int32)


def _bucket_tiles():
    table = _t5_bucket_table(2 * BLK)
    s_idx = np.arange(BLK)[:, None]
    t_idx = np.arange(BLK)[None, :]
    own = table[np.maximum(t_idx - s_idx, 0)]
    prev = table[t_idx - s_idx + BLK]
    return np.stack([own, prev]).astype(np.int32)


def _suffix_matrix():
    t = (np.arange(HALF)[None, :] >= np.arange(HALF)[:, None]).astype(np.float32)
    return np.concatenate([t, t], axis=1)


def kernel(x, p, ln_mix_g, w_in, w_gate, b_gate, w_branch_sb, w_branch_moba, w_out, rel_bias, ln_ffn_g,
           w_ffn_gate, w_ffn_up, w_ffn_down, ln_ple_g, w_ple_gate, w_ple_proj, final_g):
    depth = w_in.shape[0]
    assert depth == 1, "the final RMSNorm is fused into the last layer's kernel"
    b, s, d = x.shape
    assert s % BLK == 0
    scale = HEAD_DIM ** -0.5
    sb_scale = scale * LOG2_E
    tri = jnp.asarray(_suffix_matrix(), BF16)
    buckets = jnp.asarray(_bucket_tiles())
    tm_in = min(1024, s)
    tm_out = min(512, b * s)
    for i in range(depth):
        w = w_in[i]
        q_sb, k_sb, v_sb, q_mb, k_mb, v_mb = (w[:, j * MIX_WIDTH:(j + 1) * MIX_WIDTH] for j in range(6))
        w_qk = jnp.concatenate([q_sb * sb_scale, k_sb, q_mb * sb_scale, k_mb], axis=1).astype(BF16)
        w_vt = jnp.concatenate([v_sb, v_mb], axis=1).T.astype(BF16)
        qk, vt, gates = _in_proj(x, ln_mix_g[i][None], w_qk, w_vt, w_gate[i].astype(BF16), b_gate[i][None], tm_in)
        o_sb = _sb_attention(qk, vt, tri)
        o_mb = _moba_attention(qk, vt, rel_bias, buckets)
        x = _out_ffn(
            x.reshape(b * s, d), o_sb.reshape(b * s, MIX_WIDTH), o_mb.reshape(b * s, MIX_WIDTH),
            gates.reshape(b * s, 2 * d), p[i].reshape(b * s, -1),
            w_branch_sb[i].astype(BF16), w_branch_moba[i].astype(BF16), w_out[i].astype(BF16),
            ln_ffn_g[i][None], w_ffn_gate[i].astype(BF16), w_ffn_up[i].astype(BF16),
            w_ffn_down[i].astype(BF16), ln_ple_g[i][None], w_ple_gate[i].astype(BF16),
            w_ple_proj[i].astype(BF16), final_g[None], tm_out).reshape(b, s, d)
    return x
```

```python
import math

import jax
import jax.numpy as jnp
import numpy as np
from jax import lax
from jax.experimental import pallas as pl
from jax.experimental.pallas import tpu as pltpu

HEAD_DIM = 64
N_HEADS = 8
MIX_WIDTH = N_HEADS * HEAD_DIM
MOBA_TOPK = 3
NUM_BUCKETS = 32
MAX_EXACT = NUM_BUCKETS // 2
MAX_DISTANCE = 128
RMS_EPS = 1e-6
NEG = -1e30
LOG2_E = 1.0 / math.log(2.0)

MXU_WIDTH = 256
BF16_SUBLANES = 16
VMEM_LIMIT_BYTES = 56 * 1024 * 1024

HEADS_PER_GROUP = MXU_WIDTH // HEAD_DIM
GROUP_WIDTH = HEADS_PER_GROUP * HEAD_DIM
N_GROUPS = N_HEADS // HEADS_PER_GROUP
BLK = 256
HALF = BLK // 2
NEAR_BLOCKS = 2
SB_SKIP_BITS = 160.0
ONES_ROWS = BF16_SUBLANES
MOBA_SKEW = 1
IN_PROJ_ROWS = 1024
OUT_FFN_ROWS = 512
ROW_SUBTILE = 256
FFN_CHUNK = MXU_WIDTH

F32 = jnp.float32
BF16 = jnp.bfloat16
NT_DIMS = (((1,), (1,)), ((), ()))


def _rms(x, g):
    return x * lax.rsqrt(jnp.mean(x * x, axis=-1, keepdims=True) + RMS_EPS) * g


def _in_proj_kernel(x_ref, g_ref, wqk_ref, wvt_ref, wg_ref, bg_ref, qk_ref, vt_ref, gate_ref):
    subs = [slice(r, r + ROW_SUBTILE) for r in range(0, x_ref.shape[0], ROW_SUBTILE)]
    h = [_rms(x_ref[rows, :], g_ref[...]).astype(BF16) for rows in subs]
    for rows, hs in zip(subs, h):
        qk_ref[rows, :] = jnp.dot(hs, wqk_ref[...], preferred_element_type=F32).astype(BF16)
    for rows, hs in zip(subs, h):
        vt_ref[:, rows] = lax.dot_general(wvt_ref[...], hs, NT_DIMS, preferred_element_type=F32).astype(BF16)
    for rows, hs in zip(subs, h):
        gl = jnp.dot(hs, wg_ref[...], preferred_element_type=F32) + bg_ref[...]
        gate_ref[rows, :] = jax.nn.sigmoid(gl).astype(BF16)


def _in_proj(x, ln_g, w_qk, w_vt, w_gate, b_gate, tm):
    b, s, d = x.shape
    n_qk, n_v, n_g = w_qk.shape[1], w_vt.shape[0], w_gate.shape[1]

    def const(shape):
        return pl.BlockSpec(shape, lambda bi, ti: (0, 0), pipeline_mode=pl.Buffered(1))

    return pl.pallas_call(
        _in_proj_kernel,
        grid=(b, s // tm),
        in_specs=[
            pl.BlockSpec((None, tm, d), lambda bi, ti: (bi, ti, 0)),
            const((1, d)),
            const((d, n_qk)),
            const((n_v, d)),
            const((d, n_g)),
            const((1, n_g)),
        ],
        out_specs=[
            pl.BlockSpec((None, tm, n_qk), lambda bi, ti: (bi, ti, 0)),
            pl.BlockSpec((None, n_v, tm), lambda bi, ti: (bi, 0, ti)),
            pl.BlockSpec((None, tm, n_g), lambda bi, ti: (bi, ti, 0)),
        ],
        out_shape=[
            jax.ShapeDtypeStruct((b, s, n_qk), BF16),
            jax.ShapeDtypeStruct((b, n_v, s), BF16),
            jax.ShapeDtypeStruct((b, s, n_g), BF16),
        ],
        compiler_params=pltpu.CompilerParams(
            dimension_semantics=("arbitrary", "arbitrary"), vmem_limit_bytes=VMEM_LIMIT_BYTES),
        name="in_proj",
    )(x, ln_g, w_qk, w_vt, w_gate, b_gate)


def _sb_kernel(q_ref, k_ref, vt_ref, tri_ref, o_ref, acc_ref, carry_ref, qm_ref, more_ref):
    n_blk = q_ref.shape[0] // BLK
    tri = tri_ref[...]
    lane_head = lax.broadcasted_iota(jnp.int32, (BLK, GROUP_WIDTH), 1) // HEAD_DIM
    heads = range(HEADS_PER_GROUP)
    aligned = lambda x: x if isinstance(x, int) else pl.multiple_of(x, BLK)

    def masked_queries(i):
        qb = q_ref[pl.ds(aligned(i * BLK), BLK), :]
        return [jnp.where(lane_head == h, qb, jnp.zeros_like(qb)) for h in heads]

    half = lambda x, c: x[c * HALF:(c + 1) * HALF]

    class Job:
        def __init__(self, h, q_masked, k0, n_kb, carry, diagonal):
            self.h, self.q, self.k0, self.n_kb, self.carry, self.diagonal = h, q_masked, k0, n_kb, carry, diagonal

        def mask(self, x):
            n_keys = self.n_kb * BLK
            s_idx = lax.broadcasted_iota(jnp.int32, (BLK, BLK), 0)
            t_idx = lax.broadcasted_iota(jnp.int32, (BLK, BLK), 1)
            last = jnp.where(s_idx < t_idx, x[n_keys - BLK:], 0.0)
            return last if self.n_kb == 1 else jnp.concatenate([x[:n_keys - BLK], last], axis=0)

    def stage_scores(job):
        keys = k_ref[pl.ds(job.k0, job.n_kb * BLK), :]
        job.z = lax.dot_general(keys, job.q, NT_DIMS, preferred_element_type=F32)

    def stage_split(job):
        neg_abs = pltpu.bitcast(pltpu.bitcast(job.z, jnp.int32) | jnp.int32(-2 ** 31), F32)
        a = jnp.maximum(job.z, 0.0) + jnp.log(1.0 + jnp.exp2(neg_abs)) * LOG2_E
        if job.diagonal:
            a = job.mask(a)
        job.hi = a.astype(BF16)
        job.lo = (a - job.hi.astype(F32)).astype(BF16)

    def stage_suffix(job):
        job.inc = [jnp.dot(tri, jnp.concatenate([half(job.hi, c), half(job.lo, c)], axis=0),
                           preferred_element_type=F32) for c in range(2 * job.n_kb)]

    def stage_weights(job):
        n_half = 2 * job.n_kb
        row, args = job.carry, [None] * n_half
        for c in reversed(range(n_half)):
            args[c] = half(job.z, c) - job.inc[c] - row
            row = row + job.inc[c][0:1, :]
        w = jnp.exp2(jnp.concatenate(args, axis=0))
        job.w = (job.mask(w) if job.diagonal else w).astype(BF16)
        job.carry = row

    def stage_pv(job):
        rows = slice(job.h * HEAD_DIM, (job.h + 1) * HEAD_DIM)
        job.pv = jnp.dot(vt_ref[rows, pl.ds(job.k0, job.n_kb * BLK)], job.w, preferred_element_type=F32)

    stages = (stage_scores, stage_split, stage_suffix, stage_weights, stage_pv)

    def live(carry):
        return jnp.min(jnp.concatenate(carry, axis=0)) < SB_SKIP_BITS

    jobs = []
    for i in range(n_blk):
        n_near = min(i + 1, NEAR_BLOCKS)
        qm = masked_queries(i)
        jobs += [Job(h, qm[h], (i + 1 - n_near) * BLK, n_near, jnp.zeros((1, BLK), F32), True) for h in heads]
    for step in range(len(jobs) + len(stages) - 1):
        for s, stage in enumerate(stages):
            if 0 <= step - s < len(jobs):
                stage(jobs[step - s])
        done = step - (len(stages) - 1)
        if done >= 0 and done % HEADS_PER_GROUP == HEADS_PER_GROUP - 1:
            i = done // HEADS_PER_GROUP
            block = jobs[done - HEADS_PER_GROUP + 1:done + 1]
            pv = jnp.concatenate([job.pv for job in block], axis=0)
            o_ref[i * BLK:(i + 1) * BLK, :] = pv.T.astype(BF16)
            if i + 1 > NEAR_BLOCKS:
                carry = [job.carry for job in block]
                acc_ref[i] = pv
                carry_ref[i] = jnp.concatenate(carry, axis=0)
                more_ref[i] = live(carry).astype(jnp.int32)

    def far_blocks(i, _):
        @pl.when(more_ref[i] != 0)
        def _():
            first = i + 1 - NEAR_BLOCKS
            qm = masked_queries(i)
            for h in heads:
                qm_ref[h] = qm[h]
            c0 = carry_ref[i]

            def cond(state):
                jj, go, _ = state
                return (jj < first) & go

            def body(state):
                jj, _, c = state
                k0 = aligned((first - 1 - jj) * BLK)
                block = [Job(h, qm_ref[h], k0, 1, c[h], False) for h in heads]
                for stage in stages:
                    for job in block:
                        stage(job)
                acc_ref[i] += jnp.concatenate([job.pv for job in block], axis=0)
                c = tuple(job.carry for job in block)
                return jj + 1, live(c), c

            lax.while_loop(cond, body, (0, True, tuple(c0[h:h + 1, :] for h in heads)))
            o_ref[pl.ds(aligned(i * BLK), BLK), :] = acc_ref[i].T.astype(BF16)

        return 0

    lax.fori_loop(NEAR_BLOCKS, n_blk, far_blocks, 0)


def _sb_attention(qk, vt, tri):
    b, s, _ = qk.shape
    return pl.pallas_call(
        _sb_kernel,
        grid=(b, N_GROUPS),
        in_specs=[
            pl.BlockSpec((None, s, GROUP_WIDTH), lambda bi, gi: (bi, 0, gi)),
            pl.BlockSpec((None, s, GROUP_WIDTH), lambda bi, gi: (bi, 0, N_GROUPS + gi)),
            pl.BlockSpec((None, GROUP_WIDTH, s), lambda bi, gi: (bi, gi, 0)),
            pl.BlockSpec((HALF, 2 * HALF), lambda bi, gi: (0, 0)),
        ],
        out_specs=pl.BlockSpec((None, s, GROUP_WIDTH), lambda bi, gi: (bi, 0, gi)),
        out_shape=jax.ShapeDtypeStruct((b, s, MIX_WIDTH), BF16),
        scratch_shapes=[
            pltpu.VMEM((s // BLK, GROUP_WIDTH, BLK), F32),
            pltpu.VMEM((s // BLK, HEADS_PER_GROUP, BLK), F32),
            pltpu.VMEM((HEADS_PER_GROUP, BLK, GROUP_WIDTH), BF16),
            pltpu.SMEM((s // BLK,), jnp.int32),
        ],
        compiler_params=pltpu.CompilerParams(
            dimension_semantics=("arbitrary", "arbitrary"), vmem_limit_bytes=VMEM_LIMIT_BYTES),
        name="sb_attn",
    )(qk, qk, vt, tri)


def _moba_kernel(bias_ref, q_ref, k_ref, vt_ref, bkt_ref, o_ref, btile_ref, madd_ref):
    gi = pl.program_id(0)
    bi = pl.program_id(1)
    s_len = q_ref.shape[0]
    n_blk = s_len // BLK
    far_bucket = NUM_BUCKETS - 1

    @pl.when(bi == 0)
    def _():
        for which in range(2):
            bkt = bkt_ref[which]
            for h in range(HEADS_PER_GROUP):
                t = jnp.zeros((BLK, BLK), F32)
                for bu in range(NUM_BUCKETS):
                    t = jnp.where(bkt == bu, bias_ref[bu, gi * HEADS_PER_GROUP + h] * LOG2_E, t)
                btile_ref[which, h] = t

    kbar = jnp.concatenate(
        [jnp.sum(k_ref[n * BLK:(n + 1) * BLK, :].astype(F32), axis=0, keepdims=True) for n in range(n_blk)],
        axis=0) * (1.0 / BLK)
    lane_head = lax.broadcasted_iota(jnp.int32, kbar.shape, 1) // HEAD_DIM
    pieces, rem = [], kbar
    for _ in range(3):
        p = rem.astype(BF16).astype(F32)
        pieces.append(p)
        rem = rem - p
    rows = [jnp.where(lane_head == h, p, 0.0) for p in pieces for h in range(HEADS_PER_GROUP)]
    r = jnp.concatenate(rows, axis=0).astype(BF16)
    rank_lo = min(MOBA_TOPK + 1, n_blk) * BLK

    def past_blocks(lo, hi):
        blk_idx = lax.broadcasted_iota(jnp.int32, (n_blk, hi - lo), 0)
        cur_blk = (lax.broadcasted_iota(jnp.int32, (n_blk, hi - lo), 1) + lo) // BLK
        return blk_idx, blk_idx < cur_blk

    all_past = jnp.where(past_blocks(0, rank_lo)[1], 0.0, NEG)
    if rank_lo < s_len:
        gt = lax.dot_general(r, q_ref[rank_lo:, :], NT_DIMS, preferred_element_type=F32)
        npc = HEADS_PER_GROUP * n_blk
        gate = gt[0:npc] + gt[npc:2 * npc] + gt[2 * npc:3 * npc]
        blk_idx_r, past_r = past_blocks(rank_lo, s_len)
    for h in range(HEADS_PER_GROUP):
        madd_ref[h, :, :rank_lo] = all_past
        if rank_lo < s_len:
            g = jnp.where(past_r, gate[h * n_blk:(h + 1) * n_blk], -jnp.inf)
            beaten_by = jnp.zeros(g.shape, jnp.int32)
            for m in range(n_blk):
                gm = g[m:m + 1, :]
                beats = (gm > g) | ((gm == g) & (m < blk_idx_r))
                beaten_by = beaten_by + beats.astype(jnp.int32)
            madd_ref[h, :, rank_lo:] = jnp.where((beaten_by < MOBA_TOPK) & past_r, 0.0, NEG)

    s_idx = lax.broadcasted_iota(jnp.int32, (BLK, BLK), 0)
    t_idx = lax.broadcasted_iota(jnp.int32, (BLK, BLK), 1)
    causal = s_idx <= t_idx
    lane_head_q = lax.broadcasted_iota(jnp.int32, (BLK, GROUP_WIDTH), 1) // HEAD_DIM

    heads = range(HEADS_PER_GROUP)
    far_bias = [bias_ref[far_bucket, gi * HEADS_PER_GROUP + h] * LOG2_E for h in heads]

    state, out_rows = {}, {}

    def stage_scores(i, h):
        qb = q_ref[i * BLK:(i + 1) * BLK, :]
        qh = jnp.where(lane_head_q == h, qb, jnp.zeros_like(qb))
        state[i, h] = lax.dot_general(k_ref[0:(i + 1) * BLK, :], qh, NT_DIMS,
                                      preferred_element_type=F32)

    def stage_max(i, h):
        z = state[i, h]
        madd = madd_ref[h, :, i * BLK:(i + 1) * BLK]
        blocks, adds = [], []
        for j in range(i + 1):
            zj = z[j * BLK:(j + 1) * BLK]
            if j == i:
                blocks.append(jnp.where(causal, zj + btile_ref[0, h], NEG))
                adds.append(None)
            elif j == i - 1:
                blocks.append(zj + btile_ref[1, h])
                adds.append(madd[j:j + 1, :])
            else:
                blocks.append(zj)
                adds.append(madd[j:j + 1, :] + far_bias[h])
        mx = None
        for blk, add in zip(blocks, adds):
            cm = jnp.max(blk, axis=0, keepdims=True)
            cm = cm if add is None else cm + add
            mx = cm if mx is None else jnp.maximum(mx, cm)
        state[i, h] = (blocks, adds, mx)

    def stage_exp(i, h):
        blocks, adds, mx = state[i, h]
        probs = [jnp.exp2(blk + ((-mx) if add is None else (add - mx))).astype(BF16)
                 for blk, add in zip(blocks, adds)]
        state[i, h] = jnp.concatenate(probs, axis=0)

    def stage_pv(i, h):
        probs = state.pop((i, h))
        rows = slice(h * HEAD_DIM, (h + 1) * HEAD_DIM)
        n_keys = (i + 1) * BLK
        vt_ones = jnp.concatenate([vt_ref[rows, 0:n_keys], jnp.ones((ONES_ROWS, n_keys), BF16)], axis=0)
        pv = jnp.dot(vt_ones, probs, preferred_element_type=F32)
        lsum = pv[HEAD_DIM:HEAD_DIM + 1, :]
        out_rows.setdefault(i, []).append(pv[:HEAD_DIM] * (1.0 / lsum))
        if h == HEADS_PER_GROUP - 1:
            o_ref[i * BLK:(i + 1) * BLK, :] = jnp.concatenate(out_rows.pop(i), axis=0).T.astype(BF16)

    items = [(i, h) for i in range(n_blk) for h in heads]
    stages = (stage_scores, stage_max, stage_exp, stage_pv)
    for step in range(len(items) + (len(stages) - 1) * MOBA_SKEW):
        for s, stage in enumerate(stages):
            if 0 <= step - s * MOBA_SKEW < len(items):
                stage(*items[step - s * MOBA_SKEW])


def _moba_attention(qk, vt, rel_bias, buckets):
    b, s, _ = qk.shape
    n_blk = s // BLK
    return pl.pallas_call(
        _moba_kernel,
        grid=(N_GROUPS, b),
        in_specs=[
            pl.BlockSpec(memory_space=pltpu.SMEM),
            pl.BlockSpec((None, s, GROUP_WIDTH), lambda gi, bi: (bi, 0, 2 * N_GROUPS + gi)),
            pl.BlockSpec((None, s, GROUP_WIDTH), lambda gi, bi: (bi, 0, 3 * N_GROUPS + gi)),
            pl.BlockSpec((None, GROUP_WIDTH, s), lambda gi, bi: (bi, N_GROUPS + gi, 0)),
            pl.BlockSpec((2, BLK, BLK), lambda gi, bi: (0, 0, 0)),
        ],
        out_specs=pl.BlockSpec((None, s, GROUP_WIDTH), lambda gi, bi: (bi, 0, gi)),
        out_shape=jax.ShapeDtypeStruct((b, s, MIX_WIDTH), BF16),
        scratch_shapes=[
            pltpu.VMEM((2, HEADS_PER_GROUP, BLK, BLK), F32),
            pltpu.VMEM((HEADS_PER_GROUP, n_blk, s), F32),
        ],
        compiler_params=pltpu.CompilerParams(
            dimension_semantics=("arbitrary", "arbitrary"), vmem_limit_bytes=VMEM_LIMIT_BYTES),
        name="moba_attn",
    )(rel_bias, qk, qk, vt, buckets)


def _out_ffn_kernel(x_ref, osb_ref, omb_ref, gate_ref, p_ref, wbs_ref, wbm_ref, wo_ref, lnf_ref,
                    wfg_ref, wfu_ref, wfd_ref, lnp_ref, wpg_ref, wpp_ref, fin_ref, o_ref, hid_ref):
    tm, d = x_ref.shape
    ffn = wfg_ref.shape[1]
    subs = [slice(r, r + ROW_SUBTILE) for r in range(0, tm, ROW_SUBTILE)]
    y = [(jnp.dot(osb_ref[rows, :], wbs_ref[...], preferred_element_type=F32),
          jnp.dot(omb_ref[rows, :], wbm_ref[...], preferred_element_type=F32)) for rows in subs]
    x1 = []
    for rows, (y_sb, y_mb) in zip(subs, y):
        mix = gate_ref[rows, :d].astype(F32) * y_sb + gate_ref[rows, d:].astype(F32) * y_mb
        x1.append(x_ref[rows, :] + jnp.dot(mix.astype(BF16), wo_ref[...], preferred_element_type=F32))
    h2 = [_rms(v, lnf_ref[...]).astype(BF16) for v in x1]
    for c in range(0, ffn, FFN_CHUNK):
        cols = slice(c, min(c + FFN_CHUNK, ffn))
        for rows, h in zip(subs, h2):
            a = jnp.dot(h, wfg_ref[:, cols], preferred_element_type=F32)
            u = jnp.dot(h, wfu_ref[:, cols], preferred_element_type=F32)
            hid_ref[rows, cols] = (a * jax.nn.sigmoid(a) * u).astype(BF16)
    x2 = [v + jnp.dot(hid_ref[rows, :], wfd_ref[...], preferred_element_type=F32) for rows, v in zip(subs, x1)]
    h3 = [_rms(v, lnp_ref[...]).astype(BF16) for v in x2]
    for rows, v, h in zip(subs, x2, h3):
        g_ple = jax.nn.sigmoid(jnp.dot(h, wpg_ref[...], preferred_element_type=F32))
        ple = jnp.dot(p_ref[rows, :].astype(BF16), wpp_ref[...], preferred_element_type=F32)
        o_ref[rows, :] = _rms(v + g_ple * ple, fin_ref[...])


def _out_ffn(x, o_sb, o_mb, gates, p, w_bs, w_bm, w_out, ln_ffn, w_fg, w_fu, w_fd, ln_ple, w_pg, w_pp,
             fin_g, tm):
    n, d = x.shape
    row = lambda width: pl.BlockSpec((tm, width), lambda i: (i, 0))

    def const(arr):
        return pl.BlockSpec(arr.shape, lambda i: (0, 0), pipeline_mode=pl.Buffered(1))

    weights = (w_bs, w_bm, w_out, ln_ffn, w_fg, w_fu, w_fd, ln_ple, w_pg, w_pp, fin_g)
    return pl.pallas_call(
        _out_ffn_kernel,
        grid=(n // tm,),
        in_specs=[row(d), row(o_sb.shape[1]), row(o_mb.shape[1]), row(gates.shape[1]), row(p.shape[1])]
        + [const(w) for w in weights],
        out_specs=row(d),
        out_shape=jax.ShapeDtypeStruct((n, d), F32),
        scratch_shapes=[pltpu.VMEM((tm, w_fg.shape[1]), BF16)],
        compiler_params=pltpu.CompilerParams(
            dimension_semantics=("arbitrary",), vmem_limit_bytes=VMEM_LIMIT_BYTES),
        name="out_ffn",
    )(x, o_sb, o_mb, gates, p, *weights)


def _t5_bucket_table(n):
    dist = np.arange(n)
    nf = np.maximum(dist, 1).astype(np.float32)
    large = MAX_EXACT + (np.log(nf / np.float32(MAX_EXACT)) / np.float32(math.log(MAX_DISTANCE / MAX_EXACT))
                         * np.float32(NUM_BUCKETS - MAX_EXACT)).astype(np.int32)
    large = np.minimum(large, NUM_BUCKETS - 1)
    return np.where(dist < MAX_EXACT, dist, large).astype(np.int32)


def _bucket_tiles():
    table = _t5_bucket_table(2 * BLK)
    s_idx = np.arange(BLK)[:, None]
    t_idx = np.arange(BLK)[None, :]
    own = table[np.maximum(t_idx - s_idx, 0)]
    prev = table[t_idx - s_idx + BLK]
    return np.stack([own, prev]).astype(np.int32)


def _suffix_matrix():
    t = (np.arange(HALF)[None, :] >= np.arange(HALF)[:, None]).astype(np.float32)
    return np.concatenate([t, t], axis=1)


def kernel(x, p, ln_mix_g, w_in, w_gate, b_gate, w_branch_sb, w_branch_moba, w_out, rel_bias, ln_ffn_g,
           w_ffn_gate, w_ffn_up, w_ffn_down, ln_ple_g, w_ple_gate, w_ple_proj, final_g):
    depth = w_in.shape[0]
    assert depth == 1, "the final RMSNorm is fused into the last layer's kernel"
    b, s, d = x.shape
    assert s % BLK == 0
    q_scale = HEAD_DIM ** -0.5 * LOG2_E
    tri = jnp.asarray(_suffix_matrix(), BF16)
    buckets = jnp.asarray(_bucket_tiles())
    tm_in = min(IN_PROJ_ROWS, s)
    tm_out = min(OUT_FFN_ROWS, b * s)
    assert s % tm_in == 0 and (b * s) % tm_out == 0 and tm_in % ROW_SUBTILE == 0 and tm_out % ROW_SUBTILE == 0
    for i in range(depth):
        w = w_in[i]
        q_sb, k_sb, v_sb, q_mb, k_mb, v_mb = (w[:, j * MIX_WIDTH:(j + 1) * MIX_WIDTH] for j in range(6))
        w_qk = jnp.concatenate([q_sb * q_scale, k_sb, q_mb * q_scale, k_mb], axis=1).astype(BF16)
        w_vt = jnp.concatenate([v_sb, v_mb], axis=1).T.astype(BF16)
        qk, vt, gates = _in_proj(x, ln_mix_g[i][None], w_qk, w_vt, w_gate[i].astype(BF16), b_gate[i][None], tm_in)
        o_sb = _sb_attention(qk, vt, tri)
        o_mb = _moba_attention(qk, vt, rel_bias, buckets)
        x = _out_ffn(
            x.reshape(b * s, d), o_sb.reshape(b * s, MIX_WIDTH), o_mb.reshape(b * s, MIX_WIDTH),
            gates.reshape(b * s, 2 * d), p[i].reshape(b * s, -1),
            w_branch_sb[i].astype(BF16), w_branch_moba[i].astype(BF16), w_out[i].astype(BF16),
            ln_ffn_g[i][None], w_ffn_gate[i].astype(BF16), w_ffn_up[i].astype(BF16),
            w_ffn_down[i].astype(BF16), ln_ple_g[i][None], w_ple_gate[i].astype(BF16),
            w_ple_proj[i].astype(BF16), final_g[None], tm_out).reshape(b, s, d)
    return x
```

```python
import math

import jax
import jax.numpy as jnp
import numpy as np
from jax import lax
from jax.experimental import pallas as pl
from jax.experimental.pallas import tpu as pltpu

HEAD_DIM = 64
N_HEADS = 8
MIX_WIDTH = N_HEADS * HEAD_DIM
MOBA_TOPK = 3
NUM_BUCKETS = 32
MAX_EXACT = NUM_BUCKETS // 2
MAX_DISTANCE = 128
RMS_EPS = 1e-6
NEG = -1e30
LOG2_E = 1.0 / math.log(2.0)

MXU_WIDTH = 256
BF16_SUBLANES = 16
VMEM_LIMIT_BYTES = 56 * 1024 * 1024

HEADS_PER_GROUP = MXU_WIDTH // HEAD_DIM
GROUP_WIDTH = HEADS_PER_GROUP * HEAD_DIM
N_GROUPS = N_HEADS // HEADS_PER_GROUP
BLK = 256
HALF = BLK // 2
NEAR_BLOCKS = 2
SB_SKIP_BITS = 160.0
ONES_ROWS = BF16_SUBLANES
MOBA_SKEW = 1
IN_PROJ_ROWS = 1024
OUT_FFN_ROWS = 512
ROW_SUBTILE = 256
FFN_CHUNK = MXU_WIDTH

F32 = jnp.float32
BF16 = jnp.bfloat16
NT_DIMS = (((1,), (1,)), ((), ()))


def _rms(x, g):
    return x * lax.rsqrt(jnp.mean(x * x, axis=-1, keepdims=True) + RMS_EPS) * g


def _in_proj_kernel(x_ref, g_ref, wqk_ref, wvt_ref, wg_ref, bg_ref, qk_ref, vt_ref, gate_ref):
    subs = [slice(r, r + ROW_SUBTILE) for r in range(0, x_ref.shape[0], ROW_SUBTILE)]
    h = [_rms(x_ref[rows, :], g_ref[...]).astype(BF16) for rows in subs]
    for rows, hs in zip(subs, h):
        qk_ref[rows, :] = jnp.dot(hs, wqk_ref[...], preferred_element_type=F32).astype(BF16)
    for rows, hs in zip(subs, h):
        vt_ref[:, rows] = lax.dot_general(wvt_ref[...], hs, NT_DIMS, preferred_element_type=F32).astype(BF16)
    for rows, hs in zip(subs, h):
        gl = jnp.dot(hs, wg_ref[...], preferred_element_type=F32) + bg_ref[...]
        gate_ref[rows, :] = jax.nn.sigmoid(gl).astype(BF16)


def _in_proj(x, ln_g, w_qk, w_vt, w_gate, b_gate, tm):
    b, s, d = x.shape
    n_qk, n_v, n_g = w_qk.shape[1], w_vt.shape[0], w_gate.shape[1]

    def const(shape):
        return pl.BlockSpec(shape, lambda bi, ti: (0, 0), pipeline_mode=pl.Buffered(1))

    return pl.pallas_call(
        _in_proj_kernel,
        grid=(b, s // tm),
        in_specs=[
            pl.BlockSpec((None, tm, d), lambda bi, ti: (bi, ti, 0)),
            const((1, d)),
            const((d, n_qk)),
            const((n_v, d)),
            const((d, n_g)),
            const((1, n_g)),
        ],
        out_specs=[
            pl.BlockSpec((None, tm, n_qk), lambda bi, ti: (bi, ti, 0)),
            pl.BlockSpec((None, n_v, tm), lambda bi, ti: (bi, 0, ti)),
            pl.BlockSpec((None, tm, n_g), lambda bi, ti: (bi, ti, 0)),
        ],
        out_shape=[
            jax.ShapeDtypeStruct((b, s, n_qk), BF16),
            jax.ShapeDtypeStruct((b, n_v, s), BF16),
            jax.ShapeDtypeStruct((b, s, n_g), BF16),
        ],
        compiler_params=pltpu.CompilerParams(
            dimension_semantics=("arbitrary", "arbitrary"), vmem_limit_bytes=VMEM_LIMIT_BYTES),
        name="in_proj",
    )(x, ln_g, w_qk, w_vt, w_gate, b_gate)


def _sb_kernel(q_ref, k_ref, vt_ref, tri_ref, o_ref, acc_ref, carry_ref, qm_ref, more_ref):
    n_blk = q_ref.shape[0] // BLK
    tri = tri_ref[...]
    lane_head = lax.broadcasted_iota(jnp.int32, (BLK, GROUP_WIDTH), 1) // HEAD_DIM
    heads = range(HEADS_PER_GROUP)
    aligned = lambda x: x if isinstance(x, int) else pl.multiple_of(x, BLK)

    def masked_queries(i):
        qb = q_ref[pl.ds(aligned(i * BLK), BLK), :]
        return [jnp.where(lane_head == h, qb, jnp.zeros_like(qb)) for h in heads]

    half = lambda x, c: x[c * HALF:(c + 1) * HALF]

    class Job:
        def __init__(self, h, q_masked, k0, n_kb, carry, diagonal):
            self.h, self.q, self.k0, self.n_kb, self.carry, self.diagonal = h, q_masked, k0, n_kb, carry, diagonal

        def mask(self, x):
            n_keys = self.n_kb * BLK
            s_idx = lax.broadcasted_iota(jnp.int32, (BLK, BLK), 0)
            t_idx = lax.broadcasted_iota(jnp.int32, (BLK, BLK), 1)
            last = jnp.where(s_idx < t_idx, x[n_keys - BLK:], 0.0)
            return last if self.n_kb == 1 else jnp.concatenate([x[:n_keys - BLK], last], axis=0)

    def stage_scores(job):
        keys = k_ref[pl.ds(job.k0, job.n_kb * BLK), :]
        job.z = lax.dot_general(keys, job.q, NT_DIMS, preferred_element_type=F32)

    def stage_split(job):
        neg_abs = pltpu.bitcast(pltpu.bitcast(job.z, jnp.int32) | jnp.int32(-2 ** 31), F32)
        a = jnp.maximum(job.z, 0.0) + jnp.log(1.0 + jnp.exp2(neg_abs)) * LOG2_E
        if job.diagonal:
            a = job.mask(a)
        job.hi = a.astype(BF16)
        job.lo = (a - job.hi.astype(F32)).astype(BF16)

    def stage_suffix(job):
        job.inc = [jnp.dot(tri, jnp.concatenate([half(job.hi, c), half(job.lo, c)], axis=0),
                           preferred_element_type=F32) for c in range(2 * job.n_kb)]

    def stage_weights(job):
        n_half = 2 * job.n_kb
        row, args = job.carry, [None] * n_half
        for c in reversed(range(n_half)):
            args[c] = half(job.z, c) - job.inc[c] - row
            row = row + job.inc[c][0:1, :]
        w = jnp.exp2(jnp.concatenate(args, axis=0))
        job.w = (job.mask(w) if job.diagonal else w).astype(BF16)
        job.carry = row

    def stage_pv(job):
        rows = slice(job.h * HEAD_DIM, (job.h + 1) * HEAD_DIM)
        job.pv = jnp.dot(vt_ref[rows, pl.ds(job.k0, job.n_kb * BLK)], job.w, preferred_element_type=F32)

    stages = (stage_scores, stage_split, stage_suffix, stage_weights, stage_pv)

    def live(carry):
        return jnp.min(jnp.concatenate(carry, axis=0)) < SB_SKIP_BITS

    jobs = []
    for i in range(n_blk):
        n_near = min(i + 1, NEAR_BLOCKS)
        qm = masked_queries(i)
        jobs += [Job(h, qm[h], (i + 1 - n_near) * BLK, n_near, jnp.zeros((1, BLK), F32), True) for h in heads]
    for step in range(len(jobs) + len(stages) - 1):
        for s, stage in enumerate(stages):
            if 0 <= step - s < len(jobs):
                stage(jobs[step - s])
        done = step - (len(stages) - 1)
        if done >= 0 and done % HEADS_PER_GROUP == HEADS_PER_GROUP - 1:
            i = done // HEADS_PER_GROUP
            block = jobs[done - HEADS_PER_GROUP + 1:done + 1]
            pv = jnp.concatenate([job.pv for job in block], axis=0)
            o_ref[i * BLK:(i + 1) * BLK, :] = pv.T.astype(BF16)
            if i + 1 > NEAR_BLOCKS:
                carry = [job.carry for job in block]
                acc_ref[i] = pv
                carry_ref[i] = jnp.concatenate(carry, axis=0)
                more_ref[i] = live(carry).astype(jnp.int32)

    def far_blocks(i, _):
        @pl.when(more_ref[i] != 0)
        def _():
            first = i + 1 - NEAR_BLOCKS
            qm = masked_queries(i)
            for h in heads:
                qm_ref[h] = qm[h]
            c0 = carry_ref[i]

            def cond(state):
                jj, go, _ = state
                return (jj < first) & go

            def body(state):
                jj, _, c = state
                k0 = aligned((first - 1 - jj) * BLK)
                block = [Job(h, qm_ref[h], k0, 1, c[h], False) for h in heads]
                for stage in stages:
                    for job in block:
                        stage(job)
                acc_ref[i] += jnp.concatenate([job.pv for job in block], axis=0)
                c = tuple(job.carry for job in block)
                return jj + 1, live(c), c

            lax.while_loop(cond, body, (0, True, tuple(c0[h:h + 1, :] for h in heads)))
            o_ref[pl.ds(aligned(i * BLK), BLK), :] = acc_ref[i].T.astype(BF16)

        return 0

    lax.fori_loop(NEAR_BLOCKS, n_blk, far_blocks, 0)


def _sb_attention(qk, vt, tri):
    b, s, _ = qk.shape
    return pl.pallas_call(
        _sb_kernel,
        grid=(b, N_GROUPS),
        in_specs=[
            pl.BlockSpec((None, s, GROUP_WIDTH), lambda bi, gi: (bi, 0, gi)),
            pl.BlockSpec((None, s, GROUP_WIDTH), lambda bi, gi: (bi, 0, N_GROUPS + gi)),
            pl.BlockSpec((None, GROUP_WIDTH, s), lambda bi, gi: (bi, gi, 0)),
            pl.BlockSpec((HALF, 2 * HALF), lambda bi, gi: (0, 0)),
        ],
        out_specs=pl.BlockSpec((None, s, GROUP_WIDTH), lambda bi, gi: (bi, 0, gi)),
        out_shape=jax.ShapeDtypeStruct((b, s, MIX_WIDTH), BF16),
        scratch_shapes=[
            pltpu.VMEM((s // BLK, GROUP_WIDTH, BLK), F32),
            pltpu.VMEM((s // BLK, HEADS_PER_GROUP, BLK), F32),
            pltpu.VMEM((HEADS_PER_GROUP, BLK, GROUP_WIDTH), BF16),
            pltpu.SMEM((s // BLK,), jnp.int32),
        ],
        compiler_params=pltpu.CompilerParams(
            dimension_semantics=("arbitrary", "arbitrary"), vmem_limit_bytes=VMEM_LIMIT_BYTES),
        name="sb_attn",
    )(qk, qk, vt, tri)


def _moba_kernel(bias_ref, q_ref, k_ref, vt_ref, bkt_ref, o_ref, btile_ref, madd_ref):
    gi = pl.program_id(0)
    bi = pl.program_id(1)
    s_len = q_ref.shape[0]
    n_blk = s_len // BLK
    far_bucket = NUM_BUCKETS - 1

    @pl.when(bi == 0)
    def _():
        for which in range(2):
            bkt = bkt_ref[which]
            for h in range(HEADS_PER_GROUP):
                t = jnp.zeros((BLK, BLK), F32)
                for bu in range(NUM_BUCKETS):
                    t = jnp.where(bkt == bu, bias_ref[bu, gi * HEADS_PER_GROUP + h] * LOG2_E, t)
                btile_ref[which, h] = t

    kbar = jnp.concatenate(
        [jnp.sum(k_ref[n * BLK:(n + 1) * BLK, :].astype(F32), axis=0, keepdims=True) for n in range(n_blk)],
        axis=0) * (1.0 / BLK)
    lane_head = lax.broadcasted_iota(jnp.int32, kbar.shape, 1) // HEAD_DIM
    pieces, rem = [], kbar
    for _ in range(3):
        p = rem.astype(BF16).astype(F32)
        pieces.append(p)
        rem = rem - p
    rows = [jnp.where(lane_head == h, p, 0.0) for p in pieces for h in range(HEADS_PER_GROUP)]
    r = jnp.concatenate(rows, axis=0).astype(BF16)
    rank_lo = min(MOBA_TOPK + 1, n_blk) * BLK

    def past_blocks(lo, hi):
        blk_idx = lax.broadcasted_iota(jnp.int32, (n_blk, hi - lo), 0)
        cur_blk = (lax.broadcasted_iota(jnp.int32, (n_blk, hi - lo), 1) + lo) // BLK
        return blk_idx, blk_idx < cur_blk

    all_past = jnp.where(past_blocks(0, rank_lo)[1], 0.0, NEG)
    if rank_lo < s_len:
        gt = lax.dot_general(r, q_ref[rank_lo:, :], NT_DIMS, preferred_element_type=F32)
        npc = HEADS_PER_GROUP * n_blk
        gate = gt[0:npc] + gt[npc:2 * npc] + gt[2 * npc:3 * npc]
        blk_idx_r, past_r = past_blocks(rank_lo, s_len)
    for h in range(HEADS_PER_GROUP):
        madd_ref[h, :, :rank_lo] = all_past
        if rank_lo < s_len:
            g = jnp.where(past_r, gate[h * n_blk:(h + 1) * n_blk], -jnp.inf)
            beaten_by = jnp.zeros(g.shape, jnp.int32)
            for m in range(n_blk):
                gm = g[m:m + 1, :]
                beats = (gm > g) | ((gm == g) & (m < blk_idx_r))
                beaten_by = beaten_by + beats.astype(jnp.int32)
            madd_ref[h, :, rank_lo:] = jnp.where((beaten_by < MOBA_TOPK) & past_r, 0.0, NEG)

    s_idx = lax.broadcasted_iota(jnp.int32, (BLK, BLK), 0)
    t_idx = lax.broadcasted_iota(jnp.int32, (BLK, BLK), 1)
    causal = s_idx <= t_idx
    lane_head_q = lax.broadcasted_iota(jnp.int32, (BLK, GROUP_WIDTH), 1) // HEAD_DIM

    heads = range(HEADS_PER_GROUP)
    far_bias = [bias_ref[far_bucket, gi * HEADS_PER_GROUP + h] * LOG2_E for h in heads]

    scores, state, out_rows = {}, {}, {}

    def stage_scores(i, h):
        qb = q_ref[i * BLK:(i + 1) * BLK, :]
        qh = jnp.where(lane_head_q == h, qb, jnp.zeros_like(qb))
        scores[i, h] = lax.dot_general(k_ref[0:(i + 1) * BLK, :], qh, NT_DIMS,
                                       preferred_element_type=F32)

    def softmax_half(i, h, part):
        lanes = slice(part * HALF, (part + 1) * HALF)
        z = scores[i, h][:, lanes]
        madd = madd_ref[h, :, i * BLK + part * HALF:i * BLK + (part + 1) * HALF]
        blocks, adds = [], []
        for j in range(i + 1):
            zj = z[j * BLK:(j + 1) * BLK]
            if j == i:
                blocks.append(jnp.where(causal[:, lanes], zj + btile_ref[0, h, :, lanes], NEG))
                adds.append(None)
            elif j == i - 1:
                blocks.append(zj + btile_ref[1, h, :, lanes])
                adds.append(madd[j:j + 1, :])
            else:
                blocks.append(zj)
                adds.append(madd[j:j + 1, :] + far_bias[h])
        mx = None
        for blk, add in zip(blocks, adds):
            cm = jnp.max(blk, axis=0, keepdims=True)
            cm = cm if add is None else cm + add
            mx = cm if mx is None else jnp.maximum(mx, cm)
        probs = [jnp.exp2(blk + ((-mx) if add is None else (add - mx))).astype(BF16)
                 for blk, add in zip(blocks, adds)]
        return jnp.concatenate(probs, axis=0)

    def stage_first_half(i, h):
        state[i, h] = softmax_half(i, h, 0)

    def stage_second_half(i, h):
        state[i, h] = jnp.concatenate([state[i, h], softmax_half(i, h, 1)], axis=1)
        del scores[i, h]

    def stage_pv(i, h):
        probs = state.pop((i, h))
        rows = slice(h * HEAD_DIM, (h + 1) * HEAD_DIM)
        n_keys = (i + 1) * BLK
        vt_ones = jnp.concatenate([vt_ref[rows, 0:n_keys], jnp.ones((ONES_ROWS, n_keys), BF16)], axis=0)
        pv = jnp.dot(vt_ones, probs, preferred_element_type=F32)
        lsum = pv[HEAD_DIM:HEAD_DIM + 1, :]
        out_rows.setdefault(i, []).append(pv[:HEAD_DIM] * (1.0 / lsum))
        if h == HEADS_PER_GROUP - 1:
            o_ref[i * BLK:(i + 1) * BLK, :] = jnp.concatenate(out_rows.pop(i), axis=0).T.astype(BF16)

    items = [(i, h) for i in range(n_blk) for h in heads]
    stages = (stage_scores, stage_first_half, stage_second_half, stage_pv)
    for step in range(len(items) + (len(stages) - 1) * MOBA_SKEW):
        for s, stage in enumerate(stages):
            if 0 <= step - s * MOBA_SKEW < len(items):
                stage(*items[step - s * MOBA_SKEW])


def _moba_attention(qk, vt, rel_bias, buckets):
    b, s, _ = qk.shape
    n_blk = s // BLK
    return pl.pallas_call(
        _moba_kernel,
        grid=(N_GROUPS, b),
        in_specs=[
            pl.BlockSpec(memory_space=pltpu.SMEM),
            pl.BlockSpec((None, s, GROUP_WIDTH), lambda gi, bi: (bi, 0, 2 * N_GROUPS + gi)),
            pl.BlockSpec((None, s, GROUP_WIDTH), lambda gi, bi: (bi, 0, 3 * N_GROUPS + gi)),
            pl.BlockSpec((None, GROUP_WIDTH, s), lambda gi, bi: (bi, N_GROUPS + gi, 0)),
            pl.BlockSpec((2, BLK, BLK), lambda gi, bi: (0, 0, 0)),
        ],
        out_specs=pl.BlockSpec((None, s, GROUP_WIDTH), lambda gi, bi: (bi, 0, gi)),
        out_shape=jax.ShapeDtypeStruct((b, s, MIX_WIDTH), BF16),
        scratch_shapes=[
            pltpu.VMEM((2, HEADS_PER_GROUP, BLK, BLK), F32),
            pltpu.VMEM((HEADS_PER_GROUP, n_blk, s), F32),
        ],
        compiler_params=pltpu.CompilerParams(
            dimension_semantics=("arbitrary", "arbitrary"), vmem_limit_bytes=VMEM_LIMIT_BYTES),
        name="moba_attn",
    )(rel_bias, qk, qk, vt, buckets)


def _out_ffn_kernel(x_ref, osb_ref, omb_ref, gate_ref, p_ref, wbs_ref, wbm_ref, wo_ref, lnf_ref,
                    wfg_ref, wfu_ref, wfd_ref, lnp_ref, wpg_ref, wpp_ref, fin_ref, o_ref, hid_ref):
    tm, d = x_ref.shape
    ffn = wfg_ref.shape[1]
    subs = [slice(r, r + ROW_SUBTILE) for r in range(0, tm, ROW_SUBTILE)]
    y = [(jnp.dot(osb_ref[rows, :], wbs_ref[...], preferred_element_type=F32),
          jnp.dot(omb_ref[rows, :], wbm_ref[...], preferred_element_type=F32)) for rows in subs]
    x1 = []
    for rows, (y_sb, y_mb) in zip(subs, y):
        mix = gate_ref[rows, :d].astype(F32) * y_sb + gate_ref[rows, d:].astype(F32) * y_mb
        x1.append(x_ref[rows, :] + jnp.dot(mix.astype(BF16), wo_ref[...], preferred_element_type=F32))
    h2 = [_rms(v, lnf_ref[...]).astype(BF16) for v in x1]
    for c in range(0, ffn, FFN_CHUNK):
        cols = slice(c, min(c + FFN_CHUNK, ffn))
        for rows, h in zip(subs, h2):
            a = jnp.dot(h, wfg_ref[:, cols], preferred_element_type=F32)
            u = jnp.dot(h, wfu_ref[:, cols], preferred_element_type=F32)
            hid_ref[rows, cols] = (a * jax.nn.sigmoid(a) * u).astype(BF16)
    x2 = [v + jnp.dot(hid_ref[rows, :], wfd_ref[...], preferred_element_type=F32) for rows, v in zip(subs, x1)]
    h3 = [_rms(v, lnp_ref[...]).astype(BF16) for v in x2]
    for rows, v, h in zip(subs, x2, h3):
        g_ple = jax.nn.sigmoid(jnp.dot(h, wpg_ref[...], preferred_element_type=F32))
        ple = jnp.dot(p_ref[rows, :].astype(BF16), wpp_ref[...], preferred_element_type=F32)
        o_ref[rows, :] = _rms(v + g_ple * ple, fin_ref[...])


def _out_ffn(x, o_sb, o_mb, gates, p, w_bs, w_bm, w_out, ln_ffn, w_fg, w_fu, w_fd, ln_ple, w_pg, w_pp,
             fin_g, tm):
    n, d = x.shape
    row = lambda width: pl.BlockSpec((tm, width), lambda i: (i, 0))

    def const(arr):
        return pl.BlockSpec(arr.shape, lambda i: (0, 0), pipeline_mode=pl.Buffered(1))

    weights = (w_bs, w_bm, w_out, ln_ffn, w_fg, w_fu, w_fd, ln_ple, w_pg, w_pp, fin_g)
    return pl.pallas_call(
        _out_ffn_kernel,
        grid=(n // tm,),
        in_specs=[row(d), row(o_sb.shape[1]), row(o_mb.shape[1]), row(gates.shape[1]), row(p.shape[1])]
        + [const(w) for w in weights],
        out_specs=row(d),
        out_shape=jax.ShapeDtypeStruct((n, d), F32),
        scratch_shapes=[pltpu.VMEM((tm, w_fg.shape[1]), BF16)],
        compiler_params=pltpu.CompilerParams(
            dimension_semantics=("arbitrary",), vmem_limit_bytes=VMEM_LIMIT_BYTES),
        name="out_ffn",
    )(x, o_sb, o_mb, gates, p, *weights)


def _t5_bucket_table(n):
    dist = np.arange(n)
    nf = np.maximum(dist, 1).astype(np.float32)
    large = MAX_EXACT + (np.log(nf / np.float32(MAX_EXACT)) / np.float32(math.log(MAX_DISTANCE / MAX_EXACT))
                         * np.float32(NUM_BUCKETS - MAX_EXACT)).astype(np.int32)
    large = np.minimum(large, NUM_BUCKETS - 1)
    return np.where(dist < MAX_EXACT, dist, large).astype(np.int32)


def _bucket_tiles():
    table = _t5_bucket_table(2 * BLK)
    s_idx = np.arange(BLK)[:, None]
    t_idx = np.arange(BLK)[None, :]
    own = table[np.maximum(t_idx - s_idx, 0)]
    prev = table[t_idx - s_idx + BLK]
    return np.stack([own, prev]).astype(np.int32)


def _suffix_matrix():
    t = (np.arange(HALF)[None, :] >= np.arange(HALF)[:, None]).astype(np.float32)
    return np.concatenate([t, t], axis=1)


def kernel(x, p, ln_mix_g, w_in, w_gate, b_gate, w_branch_sb, w_branch_moba, w_out, rel_bias, ln_ffn_g,
           w_ffn_gate, w_ffn_up, w_ffn_down, ln_ple_g, w_ple_gate, w_ple_proj, final_g):
    depth = w_in.shape[0]
    assert depth == 1, "the final RMSNorm is fused into the last layer's kernel"
    b, s, d = x.shape
    assert s % BLK == 0
    q_scale = HEAD_DIM ** -0.5 * LOG2_E
    tri = jnp.asarray(_suffix_matrix(), BF16)
    buckets = jnp.asarray(_bucket_tiles())
    tm_in = min(IN_PROJ_ROWS, s)
    tm_out = min(OUT_FFN_ROWS, b * s)
    assert s % tm_in == 0 and (b * s) % tm_out == 0 and tm_in % ROW_SUBTILE == 0 and tm_out % ROW_SUBTILE == 0
    for i in range(depth):
        w = w_in[i]
        q_sb, k_sb, v_sb, q_mb, k_mb, v_mb = (w[:, j * MIX_WIDTH:(j + 1) * MIX_WIDTH] for j in range(6))
        w_qk = jnp.concatenate([q_sb * q_scale, k_sb, q_mb * q_scale, k_mb], axis=1).astype(BF16)
        w_vt = jnp.concatenate([v_sb, v_mb], axis=1).T.astype(BF16)
        qk, vt, gates = _in_proj(x, ln_mix_g[i][None], w_qk, w_vt, w_gate[i].astype(BF16), b_gate[i][None], tm_in)
        o_sb = _sb_attention(qk, vt, tri)
        o_mb = _moba_attention(qk, vt, rel_bias, buckets)
        x = _out_ffn(
            x.reshape(b * s, d), o_sb.reshape(b * s, MIX_WIDTH), o_mb.reshape(b * s, MIX_WIDTH),
            gates.reshape(b * s, 2 * d), p[i].reshape(b * s, -1),
            w_branch_sb[i].astype(BF16), w_branch_moba[i].astype(BF16), w_out[i].astype(BF16),
            ln_ffn_g[i][None], w_ffn_gate[i].astype(BF16), w_ffn_up[i].astype(BF16),
            w_ffn_down[i].astype(BF16), ln_ple_g[i][None], w_ple_gate[i].astype(BF16),
            w_ple_proj[i].astype(BF16), final_g[None], tm_out).reshape(b, s, d)
    return x
```
